```python
import jax, jax.numpy as jnp
from jax import lax
import numpy as np

D_MODEL = 1024
BATCH = 8
SEQ = 2048
DEPTH = 4

CTX_LEN = 256
GRID_W = 64
N_MIXERS = 3
ALPHA = (2 * DEPTH) ** 0.25
BETA = (8 * DEPTH) ** -0.25
LN_EPS = 1e-6
RMS_EPS = 1e-6
N_MOD = 6
FFN_HIDDEN = -(-8 * D_MODEL // (3 * 256)) * 256
POOL_WINDOWS = (2, 4, 8, 16)
N_POOL_GROUPS = len(POOL_WINDOWS)
POOL_GROUP_DIM = D_MODEL // N_POOL_GROUPS
HGRN_HEAD_DIM = 128
HGRN_HEADS = D_MODEL // HGRN_HEAD_DIM
HGRN_CHUNK = 64
NA_HEAD_DIM = 64
NA_HEADS = D_MODEL // NA_HEAD_DIM
NA_ROWS = 8
NA_COLS = 16
NA_QBLOCK = 16
NA_KBAND = 32
N_POOL_LAYERS = len(range(0, DEPTH, N_MIXERS))
N_HGRN_LAYERS = len(range(1, DEPTH, N_MIXERS))
N_NA_LAYERS = len(range(2, DEPTH, N_MIXERS))

kernel_name = 'hybrid_pool_hgrn2_natten_dit_trunk'


def layer_norm(x, g, b):
    xf = x.astype(jnp.float32)
    mu = jnp.mean(xf, axis=-1, keepdims=True)
    var = jnp.mean(jnp.square(xf - mu), axis=-1, keepdims=True)
    return ((xf - mu) * lax.rsqrt(var + LN_EPS) * g + b).astype(x.dtype)


def swiglu(h, w_in, w_out):
    gt, up = jnp.split(h @ w_in, 2, axis=-1)
    return (jax.nn.silu(gt) * up) @ w_out


def centred_window_mean(h, w):
    L = h.shape[1]
    lo = w // 2
    hi = w - 1 - lo
    t = np.arange(L)
    start = np.clip(t - lo, 0, L)
    end = np.clip(t + hi + 1, 0, L)
    hf = h.astype(jnp.float32)
    cs = jnp.concatenate([jnp.zeros_like(hf[:, :1]), jnp.cumsum(hf, axis=1)], axis=1)
    cnt = jnp.asarray((end - start).astype(np.float32))
    return ((cs[:, end] - cs[:, start]) / cnt[None, :, None]).astype(h.dtype)


def multiscale_pool(h, w_group, scale):
    B, L, D = h.shape
    outs = []
    for g, w in enumerate(POOL_WINDOWS):
        hg = h[..., g * POOL_GROUP_DIM:(g + 1) * POOL_GROUP_DIM]
        outs.append(centred_window_mean(hg, w) - hg)
    p = jnp.stack(outs, axis=2)
    y = jnp.einsum('blgc,gcd->blgd', p, w_group).reshape(B, L, D)
    return y * scale


def pool_mixer(hx, hc, w_group, scale, need_ctx):
    ox = multiscale_pool(hx, w_group, scale)
    oc = multiscale_pool(hc, w_group, scale) if need_ctx else None
    return ox, oc


def chunk_gated_scan(q, k, v, log_f, s0):
    B, H, L, DK = q.shape
    DV = v.shape[-1]
    C = HGRN_CHUNK
    n = L // C

    def chunks(a):
        return jnp.moveaxis(a.reshape(B, H, n, C, a.shape[-1]), 2, 0)

    lower = jnp.asarray(np.tril(np.ones((C, C), dtype=bool)))

    def step(s, inp):
        q_c, k_c, v_c, g_c = inp
        b = jnp.cumsum(g_c, axis=2)
        rel = jnp.where(lower[:, :, None], b[:, :, :, None, :] - b[:, :, None, :, :], -jnp.inf)
        a = jnp.einsum('bhtd,bhsd,bhtsd->bhts', q_c, k_c, jnp.exp(rel))
        o = jnp.einsum('bhts,bhse->bhte', a, v_c) + jnp.einsum('bhtd,bhde->bhte', q_c * jnp.exp(b), s)
        b_end = b[:, :, -1:, :]
        s_new = jnp.exp(b_end)[:, :, 0, :, None] * s + jnp.einsum('bhsd,bhse->bhde', k_c * jnp.exp(b_end - b), v_c)
        return s_new, o

    s_fin, o = lax.scan(step, s0, (chunks(q), chunks(k), chunks(v), chunks(log_f)))
    o = jnp.moveaxis(o, 0, 2).reshape(B, H, L, DV)
    return o, s_fin


def hgrn2_mixer(hx, hc, w_in, lb_fwd, lb_bwd, norm_w, w_out, need_ctx):
    def heads(a):
        B_, L_, _ = a.shape
        return a.reshape(B_, L_, HGRN_HEADS, HGRN_HEAD_DIM).transpose(0, 2, 1, 3).astype(jnp.float32)

    def project(h):
        q, v, zf, zb, g = jnp.split(h @ w_in, 5, axis=-1)
        return heads(jax.nn.silu(q)), heads(v), heads(zf), heads(zb), g

    def gate(z, lb):
        lb = lb.reshape(HGRN_HEADS, 1, HGRN_HEAD_DIM)
        f = lb + (1.0 - lb) * jax.nn.sigmoid(z)
        return 1.0 - f, jnp.log(f)

    def flip(a):
        return jnp.flip(a, axis=2)

    qx, vx, zfx, zbx, gx = project(hx)
    qc, vc, zfc, zbc, gc = project(hc)
    s0 = jnp.zeros((hc.shape[0], HGRN_HEADS, HGRN_HEAD_DIM, HGRN_HEAD_DIM), jnp.float32)
    kcf, lcf = gate(zfc, lb_fwd)
    kxf, lxf = gate(zfx, lb_fwd)
    oc_f, sc_f = chunk_gated_scan(qc, kcf, vc, lcf, s0)
    ox_f, _ = chunk_gated_scan(qx, kxf, vx, lxf, sc_f)
    kcb, lcb = gate(zbc, lb_bwd)
    kxb, lxb = gate(zbx, lb_bwd)
    oc_b, sc_b = chunk_gated_scan(flip(qc), flip(kcb), flip(vc), flip(lcb), s0)
    ox_b, _ = chunk_gated_scan(flip(qx), flip(kxb), flip(vx), flip(lxb), sc_b)

    def readout(o, g, dtype):
        B_, H_, L_, E_ = o.shape
        o = o * lax.rsqrt(jnp.mean(o * o, axis=-1, keepdims=True) + RMS_EPS)
        o = o.transpose(0, 2, 1, 3).reshape(B_, L_, H_ * E_).astype(dtype)
        return (o * norm_w * jax.nn.silu(g)) @ w_out

    ox = readout(ox_f + flip(ox_b), gx, hx.dtype)
    oc = readout(oc_f + flip(oc_b), gc, hc.dtype) if need_ctx else None
    return ox, oc


def na_mixer(hx, hc, w_qkv, rpb, w_out, need_ctx):
    B, L, D = hx.shape
    rows = L // GRID_W
    kr = min(NA_ROWS, rows)
    scale = NA_HEAD_DIM ** -0.5

    def qkv(h):
        B_, L_, _ = h.shape
        z = (h @ w_qkv).reshape(B_, L_, 3, NA_HEADS, NA_HEAD_DIM)
        return [jnp.moveaxis(z[:, :, m], 1, 2) for m in range(3)]

    qx, kx, vx = qkv(hx)
    qc, kc, vc = qkv(hc)

    def grid(a):
        return a.reshape(B, NA_HEADS, rows, GRID_W, NA_HEAD_DIM)

    qg, kg, vg = grid(qx), grid(kx), grid(vx)
    n_cb = GRID_W // NA_QBLOCK
    qcol = np.arange(GRID_W).reshape(n_cb, NA_QBLOCK)
    band_start = np.clip(np.arange(n_cb) * NA_QBLOCK - NA_COLS // 2, 0, GRID_W - NA_KBAND)
    kcol = band_start[:, None] + np.arange(NA_KBAND)
    win_start = np.clip(qcol - NA_COLS // 2, 0, GRID_W - NA_COLS)
    col_ok = (kcol[:, None, :] >= win_start[:, :, None]) & (kcol[:, None, :] < win_start[:, :, None] + NA_COLS)
    dcol_idx = np.clip(kcol[:, None, :] - qcol[:, :, None] + NA_COLS - 1, 0, 2 * NA_COLS - 2)
    col_ok = jnp.asarray(col_ok)
    n_loc = kr * NA_KBAND

    def row_block(r):
        rs = jnp.clip(r - kr // 2, 0, rows - kr)
        q_r = lax.dynamic_index_in_dim(qg, r, axis=2, keepdims=False)
        k_r = lax.dynamic_slice_in_dim(kg, rs, kr, axis=2)[:, :, :, kcol]
        v_r = lax.dynamic_slice_in_dim(vg, rs, kr, axis=2)[:, :, :, kcol]
        q_b = q_r.reshape(B, NA_HEADS, n_cb, NA_QBLOCK, NA_HEAD_DIM)
        s_loc = jnp.einsum('bhjqd,bhajkd->bhjqak', q_b, k_r).astype(jnp.float32) * scale
        drow = rs + jnp.arange(kr) - r + NA_ROWS - 1
        bias = rpb[:, drow][:, :, dcol_idx].transpose(0, 2, 3, 1, 4)
        s_loc = jnp.where(col_ok[:, :, None, :], s_loc + bias.astype(jnp.float32), -jnp.inf)
        s_loc = s_loc.reshape(B, NA_HEADS, n_cb, NA_QBLOCK, n_loc)
        s_ctx = jnp.einsum('bhjqd,bhmd->bhjqm', q_b, kc).astype(jnp.float32) * scale
        p = jax.nn.softmax(jnp.concatenate([s_loc, s_ctx], axis=-1), axis=-1).astype(vx.dtype)
        p_loc = p[..., :n_loc].reshape(B, NA_HEADS, n_cb, NA_QBLOCK, kr, NA_KBAND)
        o = jnp.einsum('bhjqak,bhajke->bhjqe', p_loc, v_r) + jnp.einsum('bhjqm,bhme->bhjqe', p[..., n_loc:], vc)
        return o.reshape(B, NA_HEADS, GRID_W, NA_HEAD_DIM)

    o = lax.map(row_block, jnp.arange(rows))
    ox = o.transpose(1, 0, 3, 2, 4).reshape(B, L, D) @ w_out
    oc = None
    if need_ctx:
        s = jnp.einsum('bhqd,bhkd->bhqk', qc, kc).astype(jnp.float32) * scale
        pc = jax.nn.softmax(s, axis=-1).astype(vc.dtype)
        occ = jnp.einsum('bhqk,bhkd->bhqd', pc, vc)
        oc = occ.transpose(0, 2, 1, 3).reshape(hc.shape[0], hc.shape[1], D) @ w_out
    return ox, oc


def setup_inputs(seed: int = 0) -> dict:
    key = jax.random.key(seed)
    ks = jax.random.split(key, 20)
    D = D_MODEL
    F = FFN_HIDDEN
    GD = POOL_GROUP_DIM

    def nrm(k, shape, s):
        return jax.random.normal(k, shape, jnp.float32) * s

    return {
        'x': nrm(ks[0], (BATCH, SEQ, D), 1.0),
        'c': nrm(ks[1], (BATCH, D), 1.0),
        'ctx': nrm(ks[2], (BATCH, CTX_LEN, D), 1.0),
        'c_ctx': nrm(ks[3], (D,), 1.0),
        'mod_w': nrm(ks[4], (DEPTH, D, N_MOD * D), 0.5 * D ** -0.5),
        'mod_b': nrm(ks[5], (DEPTH, N_MOD * D), 0.02),
        'ln_g': 1.0 + nrm(ks[6], (DEPTH, 2, D), 0.02),
        'ln_b': nrm(ks[7], (DEPTH, 2, D), 0.02),
        'ffn_w_in': nrm(ks[8], (DEPTH, D, 2 * F), D ** -0.5),
        'ffn_w_out': nrm(ks[9], (DEPTH, F, D), BETA * F ** -0.5),
        'pool_w': nrm(ks[10], (N_POOL_LAYERS, N_POOL_GROUPS, GD, GD), BETA * GD ** -0.5),
        'pool_scale': 1.0 + nrm(ks[11], (N_POOL_LAYERS, D), 0.02),
        'hgrn_w_in': nrm(ks[12], (N_HGRN_LAYERS, D, 5 * D), D ** -0.5),
        'hgrn_lb_logits': 1.0 + nrm(ks[13], (2, DEPTH, D), 0.02),
        'hgrn_norm_w': 1.0 + nrm(ks[14], (N_HGRN_LAYERS, D), 0.02),
        'hgrn_w_out': nrm(ks[15], (N_HGRN_LAYERS, D, D), BETA * D ** -0.5),
        'na_w_qkv': nrm(ks[16], (N_NA_LAYERS, D, 3 * D), D ** -0.5),
        'na_rpb': nrm(ks[17], (N_NA_LAYERS, NA_HEADS, 2 * NA_ROWS - 1, 2 * NA_COLS - 1), 0.02),
        'na_w_out': nrm(ks[18], (N_NA_LAYERS, D, D), BETA * D ** -0.5),
    }


def reference(x, c, ctx, c_ctx, mod_w, mod_b, ln_g, ln_b, ffn_w_in, ffn_w_out,
              pool_w, pool_scale, hgrn_w_in, hgrn_lb_logits, hgrn_norm_w, hgrn_w_out,
              na_w_qkv, na_rpb, na_w_out):
    p_lb = jax.nn.softmax(hgrn_lb_logits.astype(jnp.float32), axis=1)
    lower_bounds = jnp.cumsum(p_lb, axis=1) - p_lb[:, :1]
    s_c = jax.nn.silu(c)
    s_cc = jax.nn.silu(c_ctx)
    for i in range(DEPTH):
        last = i == DEPTH - 1
        kind = i % N_MIXERS
        j = i // N_MIXERS
        mod_x = s_c @ mod_w[i] + mod_b[i]
        mod_c = s_cc @ mod_w[i] + mod_b[i]
        sh1, sc1, g1, sh2, sc2, g2 = jnp.split(mod_x[:, None, :], N_MOD, axis=-1)
        csh1, csc1, cg1, csh2, csc2, cg2 = jnp.split(mod_c, N_MOD, axis=-1)
        hx = x * (1.0 + sc1) + sh1
        hc = ctx * (1.0 + csc1) + csh1 if (kind != 0 or not last) else None
        if kind == 0:
            ox, oc = pool_mixer(hx, hc, pool_w[j], pool_scale[j], not last)
        elif kind == 1:
            ox, oc = hgrn2_mixer(hx, hc, hgrn_w_in[j], lower_bounds[0, i], lower_bounds[1, i],
                                 hgrn_norm_w[j], hgrn_w_out[j], not last)
        else:
            ox, oc = na_mixer(hx, hc, na_w_qkv[j], na_rpb[j], na_w_out[j], not last)
        x = layer_norm(ALPHA * x + g1 * ox, ln_g[i, 0], ln_b[i, 0])
        x = layer_norm(ALPHA * x + g2 * swiglu(x * (1.0 + sc2) + sh2, ffn_w_in[i], ffn_w_out[i]),
                       ln_g[i, 1], ln_b[i, 1])
        if not last:
            ctx = layer_norm(ALPHA * ctx + cg1 * oc, ln_g[i, 0], ln_b[i, 0])
            ctx = layer_norm(ALPHA * ctx + cg2 * swiglu(ctx * (1.0 + csc2) + csh2, ffn_w_in[i], ffn_w_out[i]),
                             ln_g[i, 1], ln_b[i, 1])
    return x
```

```python
import functools

import numpy as np
import jax
import jax.numpy as jnp
from jax import lax
from jax.experimental import pallas as pl
from jax.experimental.pallas import tpu as pltpu

DEPTH = 4
N_MIXERS = 3
N_MOD = 6
ALPHA = (2 * DEPTH) ** 0.25
LN_EPS = 1e-6
RMS_EPS = 1e-6
POOL_WINDOWS = (2, 4, 8, 16)
HGRN_HEAD_DIM = 128
HGRN_CHUNK = 64
NA_HEAD_DIM = 64
NA_ROWS = 8
NA_COLS = 16
GRID_W = 64
NEG_BIG = -1e30

TOKEN_TILE = 256
MOD_ROWS = 16
VMEM_LIMIT = 56 * 1024 * 1024

F32 = jnp.float32
BF16 = jnp.bfloat16


def _sigmoid(x):
    return 1.0 / (1.0 + jnp.exp(-x))


def _silu(x):
    return x * _sigmoid(x)


def _layer_norm(v, g, b):
    mu = jnp.mean(v, axis=-1, keepdims=True)
    d = v - mu
    var = jnp.mean(d * d, axis=-1, keepdims=True)
    return d * lax.rsqrt(var + LN_EPS) * g + b


def _dot(a, b):
    return jnp.dot(a, b, preferred_element_type=F32)


def _dot_nt(a, b):
    return lax.dot_general(a, b, (((1,), (1,)), ((), ())), preferred_element_type=F32)


def _dot_tn(a, b):
    return lax.dot_general(a, b, (((0,), (0,)), ((), ())), preferred_element_type=F32)


def _resident(shape):
    zeros = (0,) * len(shape)
    return pl.BlockSpec(shape, lambda *_: zeros, pipeline_mode=pl.Buffered(1))


def _mod_kernel(c_ref, w_ref, b_ref, o_ref):
    s = _silu(c_ref[...]).astype(BF16)
    o_ref[0] = _dot(s, w_ref[0].astype(BF16)) + b_ref[0]


def _modulation(cc, mod_w, mod_b):
    depth, d, n = mod_w.shape
    tn = n // 4
    return pl.pallas_call(
        _mod_kernel,
        grid=(depth, n // tn),
        in_specs=[
            pl.BlockSpec((MOD_ROWS, d), lambda i, j: (0, 0)),
            pl.BlockSpec((1, d, tn), lambda i, j: (i, 0, j)),
            pl.BlockSpec((1, 1, tn), lambda i, j: (i, 0, j)),
        ],
        out_specs=pl.BlockSpec((1, MOD_ROWS, tn), lambda i, j: (i, 0, j)),
        out_shape=jax.ShapeDtypeStruct((depth, MOD_ROWS, n), F32),
        compiler_params=pltpu.CompilerParams(
            dimension_semantics=("arbitrary", "arbitrary"), vmem_limit_bytes=VMEM_LIMIT),
        name="modulation",
    )(cc, mod_w, mod_b.reshape(depth, 1, n))


def _pool_kernel(x_ref, modx_ref, modc_ref, w_ref, s_ref, o_ref, pad_ref, *, n_ctx, halo):
    g = pl.program_id(1)
    t_all = x_ref.shape[1]
    wg = w_ref[0].astype(BF16)
    scale = s_ref[...]

    def segment(r0, length, mod_ref, window):
        lo = window // 2
        hi = window - 1 - lo
        h = x_ref[0, r0:r0 + length, :] * (1.0 + mod_ref[0, 1:2, :]) + mod_ref[0, 0:1, :]
        zeros = jnp.zeros((halo, h.shape[1]), F32)
        pad_ref[0:halo, :] = zeros
        pad_ref[halo:halo + length, :] = h
        pad_ref[halo + length:2 * halo + length, :] = zeros
        tot = h
        for off in range(-lo, hi + 1):
            if off != 0:
                tot = tot + pad_ref[halo + off:halo + off + length, :]
        t = lax.broadcasted_iota(jnp.int32, h.shape, 0)
        cnt = jnp.minimum(t + hi + 1, length) - jnp.maximum(t - lo, 0)
        p = tot / cnt.astype(F32) - h
        o_ref[0, r0:r0 + length, :] = _dot(p.astype(BF16), wg) * scale

    for gi, window in enumerate(POOL_WINDOWS):
        @pl.when(g == gi)
        def _(window=window):
            segment(0, n_ctx, modc_ref, window)
            segment(n_ctx, t_all - n_ctx, modx_ref, window)


def _pool_mixer(xx, mod6, w_group, scale, n_ctx):
    b, t, d = xx.shape
    ng, gd, _ = w_group.shape
    halo = 8
    kern = functools.partial(_pool_kernel, n_ctx=n_ctx, halo=halo)
    return pl.pallas_call(
        kern,
        grid=(b, ng),
        in_specs=[
            pl.BlockSpec((1, t, gd), lambda i, g: (i, 0, g)),
            pl.BlockSpec((1, N_MOD, gd), lambda i, g: (i, 0, g)),
            pl.BlockSpec((1, N_MOD, gd), lambda i, g: (b, 0, g)),
            pl.BlockSpec((1, gd, gd), lambda i, g: (g, 0, 0)),
            pl.BlockSpec((1, gd), lambda i, g: (0, g)),
        ],
        out_specs=pl.BlockSpec((1, t, gd), lambda i, g: (i, 0, g)),
        out_shape=jax.ShapeDtypeStruct((b, t, d), F32),
        scratch_shapes=[pltpu.VMEM((t - n_ctx + 2 * halo, gd), F32)],
        compiler_params=pltpu.CompilerParams(
            dimension_semantics=("arbitrary", "arbitrary"), vmem_limit_bytes=VMEM_LIMIT),
        name="pool_mixer",
    )(xx, mod6, mod6, w_group, scale.reshape(1, d))


def _proj_kernel(x_ref, mod_ref, w_ref, o_ref, *, n_chunk):
    h = (x_ref[0] * (1.0 + mod_ref[0, 1:2, :]) + mod_ref[0, 0:1, :]).astype(BF16)
    n = w_ref.shape[1]
    for c0 in range(0, n, n_chunk):
        o_ref[0, :, c0:c0 + n_chunk] = _dot(h, w_ref[:, c0:c0 + n_chunk]).astype(o_ref.dtype)


def _mod_row_map(n_batch):
    return lambda i, j: (jnp.where(j > 0, i, n_batch), 0, 0)


def _modulated_proj(xx, mod6, w, out_dtype, n_chunk):
    b, t, d = xx.shape
    n = w.shape[1]
    kern = functools.partial(_proj_kernel, n_chunk=n_chunk)
    return pl.pallas_call(
        kern,
        grid=(b, t // TOKEN_TILE),
        in_specs=[
            pl.BlockSpec((1, TOKEN_TILE, d), lambda i, j: (i, j, 0)),
            pl.BlockSpec((1, N_MOD, d), _mod_row_map(b)),
            _resident((d, n)),
        ],
        out_specs=pl.BlockSpec((1, TOKEN_TILE, n), lambda i, j: (i, j, 0)),
        out_shape=jax.ShapeDtypeStruct((b, t, n), out_dtype),
        compiler_params=pltpu.CompilerParams(
            dimension_semantics=("arbitrary", "arbitrary"), vmem_limit_bytes=VMEM_LIMIT),
        name="modulated_proj",
    )(xx, mod6, w)


def _ffn_kernel(*refs, has_mix, ffn_hidden):
    if has_mix:
        x_ref, y_ref, mod_ref, lng_ref, lnb_ref, wmix_ref, win_ref, wout_ref, o_ref = refs
    else:
        x_ref, y_ref, mod_ref, lng_ref, lnb_ref, win_ref, wout_ref, o_ref = refs
    x = x_ref[0]
    g1 = mod_ref[0, 2:3, :]
    sh2 = mod_ref[0, 3:4, :]
    sc2 = mod_ref[0, 4:5, :]
    g2 = mod_ref[0, 5:6, :]
    if has_mix:
        ox = _dot(y_ref[0], wmix_ref[...])
    else:
        ox = y_ref[0]
    x1 = _layer_norm(ALPHA * x + g1 * ox, lng_ref[0:1, :], lnb_ref[0:1, :])
    h2 = (x1 * (1.0 + sc2) + sh2).astype(BF16)
    gu = _dot(h2, win_ref[...])
    act = (_silu(gu[:, :ffn_hidden]) * gu[:, ffn_hidden:]).astype(BF16)
    y2 = _dot(act, wout_ref[...])
    o_ref[0] = _layer_norm(ALPHA * x1 + g2 * y2, lng_ref[1:2, :], lnb_ref[1:2, :])


def _ffn_block(xx, y, mod6, ln_g, ln_b, w_mix, w_in, w_out, *, latent_only, n_ctx):
    b, t, d = xx.shape
    f = w_out.shape[0]
    off = n_ctx // TOKEN_TILE if latent_only else 0
    n_tiles = t // TOKEN_TILE - off
    has_mix = w_mix is not None
    tok = lambda i, j: (i, j + off, 0)
    in_specs = [
        pl.BlockSpec((1, TOKEN_TILE, d), tok),
        pl.BlockSpec((1, TOKEN_TILE, d), tok),
        pl.BlockSpec((1, N_MOD, d), lambda i, j: (jnp.where(j + off > 0, i, b), 0, 0)),
        _resident((2, d)),
        _resident((2, d)),
    ]
    args = [xx, y, mod6, ln_g, ln_b]
    if has_mix:
        in_specs.append(_resident((d, d)))
        args.append(w_mix)
    in_specs += [_resident((d, 2 * f)), _resident((f, d))]
    args += [w_in, w_out]
    kern = functools.partial(_ffn_kernel, has_mix=has_mix, ffn_hidden=f)
    return pl.pallas_call(
        kern,
        grid=(b, n_tiles),
        in_specs=in_specs,
        out_specs=pl.BlockSpec((1, TOKEN_TILE, d), lambda i, j: (i, j, 0)),
        out_shape=jax.ShapeDtypeStruct((b, n_tiles * TOKEN_TILE, d), F32),
        compiler_params=pltpu.CompilerParams(
            dimension_semantics=("arbitrary", "arbitrary"), vmem_limit_bytes=VMEM_LIMIT),
        name="ffn_block",
    )(*args)


def _hgrn_constants():
    c = HGRN_CHUNK
    t = np.arange(c)[:, None]
    u = np.arange(c)[None, :]
    mats = [u <= t]
    masks = [t == u]
    m = c // 2
    while m >= 1:
        ref = (t // (2 * m)) * 2 * m + m - 1
        second = (t % (2 * m)) >= m
        mats.append(np.where(second, (u > ref) & (u <= t), (u > t) & (u <= ref)))
        masks.append(((t // (2 * m)) == (u // (2 * m))) & second & ((u % (2 * m)) < m))
        m //= 2
    mats.append(u > t)
    fwd_d = np.concatenate(mats, 0)
    bwd_d = np.concatenate([a[::-1, ::-1] for a in mats], 0)
    fwd_m = np.stack([np.broadcast_to(a, (c, c)) for a in masks])
    bwd_m = np.stack([np.broadcast_to(a, (c, c))[::-1, ::-1] for a in masks])
    dmat = np.stack([fwd_d, bwd_d]).astype(np.float32)
    mask = np.stack([fwd_m, bwd_m]).astype(np.float32)
    return dmat, mask


def _hgrn_scan_kernel(q_ref, v_ref, zf_ref, zb_ref, g_ref, lbl_ref, nw_ref, dm_ref, mk_ref,
                      y_ref, of_ref, ob_ref, st_ref, *, layer, n_ctx):
    c = HGRN_CHUNK
    hd = HGRN_HEAD_DIM
    t_all = q_ref.shape[1]
    n_chunks = t_all // c
    n_ctx_chunks = n_ctx // c
    n_levels = mk_ref.shape[1]

    def lower_bound(direction):
        rows = [lbl_ref[direction, j:j + 1, :] for j in range(lbl_ref.shape[1])]
        mx = functools.reduce(jnp.maximum, rows)
        e = [jnp.exp(r - mx) for r in rows]
        tot = functools.reduce(lambda a, b_: a + b_, e)
        p = [ei / tot for ei in e]
        acc = p[0]
        for j in range(1, layer + 1):
            acc = acc + p[j]
        return acc - p[0]

    def chunk(direction, r0, z_ref, lb, end_row):
        q = _silu(q_ref[0, pl.ds(r0, c), :])
        vb = v_ref[0, pl.ds(r0, c), :].astype(BF16)
        f = lb + (1.0 - lb) * _sigmoid(z_ref[0, pl.ds(r0, c), :])
        k = 1.0 - f
        g = jnp.log(f)
        hi = g.astype(BF16)
        r1 = g - hi.astype(F32)
        mid = r1.astype(BF16)
        lo = (r1 - mid.astype(F32)).astype(BF16)
        e3 = _dot(dm_ref[direction], jnp.concatenate([hi, mid, lo], axis=1))
        x_all = jnp.exp(e3[:, :hd] + e3[:, hd:2 * hd] + e3[:, 2 * hd:])
        a = mk_ref[direction, 0] * _dot_nt(q.astype(BF16), k.astype(BF16))
        for lvl in range(1, n_levels):
            xl = x_all[lvl * c:(lvl + 1) * c]
            a = a + mk_ref[direction, lvl] * _dot_nt((q * xl).astype(BF16), (k * xl).astype(BF16))
        st = st_ref[direction]
        o = _dot(a.astype(BF16), vb) + _dot_nt((q * x_all[0:c]).astype(BF16), st.astype(BF16))
        k_end = (k * x_all[n_levels * c:(n_levels + 1) * c]).astype(BF16)
        st_ref[direction] = st * x_all[end_row:end_row + 1] + _dot_tn(vb, k_end)
        return o

    lb_f = lower_bound(0)
    lb_b = lower_bound(1)
    st_ref[...] = jnp.zeros(st_ref.shape, F32)

    def body(i, carry):
        r0f = pl.multiple_of(i * c, c)
        cb = jnp.where(i < n_ctx_chunks, n_ctx_chunks - 1 - i, n_chunks - 1 - (i - n_ctx_chunks))
        r0b = pl.multiple_of(cb * c, c)
        of_ref[pl.ds(r0f, c), :] = chunk(0, r0f, zf_ref, lb_f, c - 1)
        ob_ref[pl.ds(r0b, c), :] = chunk(1, r0b, zb_ref, lb_b, 0)
        return carry

    lax.fori_loop(0, n_chunks, body, 0)

    nw = nw_ref[...]

    def finish(j, carry):
        r0 = pl.multiple_of(j * TOKEN_TILE, TOKEN_TILE)
        o = of_ref[pl.ds(r0, TOKEN_TILE), :] + ob_ref[pl.ds(r0, TOKEN_TILE), :]
        o = o * lax.rsqrt(jnp.mean(o * o, axis=-1, keepdims=True) + RMS_EPS)
        gate = _silu(g_ref[0, pl.ds(r0, TOKEN_TILE), :])
        y_ref[0, pl.ds(r0, TOKEN_TILE), :] = (o * nw * gate).astype(y_ref.dtype)
        return carry

    lax.fori_loop(0, t_all // TOKEN_TILE, finish, 0)


def _hgrn_scan(proj, lb_logits, norm_w, *, layer, n_ctx):
    b, t, n = proj.shape
    d = n // 5
    hd = HGRN_HEAD_DIM
    heads = d // hd
    dmat, mask = _hgrn_constants()
    part = lambda p: pl.BlockSpec((1, t, hd), lambda i, h: (i, 0, p * heads + h))
    kern = functools.partial(_hgrn_scan_kernel, layer=layer, n_ctx=n_ctx)
    return pl.pallas_call(
        kern,
        grid=(b, heads),
        in_specs=[
            part(0), part(1), part(2), part(3), part(4),
            pl.BlockSpec((2, lb_logits.shape[1], hd), lambda i, h: (0, 0, h)),
            pl.BlockSpec((1, hd), lambda i, h: (0, h)),
            _resident(dmat.shape),
            _resident(mask.shape),
        ],
        out_specs=pl.BlockSpec((1, t, hd), lambda i, h: (i, 0, h)),
        out_shape=jax.ShapeDtypeStruct((b, t, d), BF16),
        scratch_shapes=[
            pltpu.VMEM((t, hd), F32),
            pltpu.VMEM((t, hd), F32),
            pltpu.VMEM((2, hd, hd), F32),
        ],
        compiler_params=pltpu.CompilerParams(
            dimension_semantics=("arbitrary", "arbitrary"), vmem_limit_bytes=VMEM_LIMIT),
        name="hgrn_scan",
    )(proj, proj, proj, proj, proj, lb_logits, norm_w.reshape(1, d),
      jnp.asarray(dmat, BF16), jnp.asarray(mask, F32))


def _na_bias_table(rpb, rows):
    kr = min(NA_ROWS, rows)
    qcol = np.arange(GRID_W)[:, None]
    kcol = np.arange(GRID_W)[None, :]
    win = np.clip(qcol - NA_COLS // 2, 0, GRID_W - NA_COLS)
    ok = (kcol >= win) & (kcol < win + NA_COLS)
    dcol = np.clip(kcol - qcol + NA_COLS - 1, 0, 2 * NA_COLS - 2)
    case = np.arange(kr)[:, None]
    a = np.arange(kr)[None, :]
    drow = np.clip(a - case + NA_ROWS - 1, 0, 2 * NA_ROWS - 2)
    bias = rpb[:, drow][:, :, :, dcol]
    bias = jnp.where(jnp.asarray(ok)[None, None, None], bias.astype(F32), NEG_BIG)
    bias = bias.transpose(0, 1, 3, 2, 4)
    return bias.reshape(rpb.shape[0], kr, GRID_W, kr * GRID_W)


def _na_kernel(q_ref, k_ref, v_ref, bias_ref, y_ref, *, n_ctx, rows):
    w = GRID_W
    kr = min(NA_ROWS, rows)
    scale = NA_HEAD_DIM ** -0.5
    first = lax.broadcasted_iota(jnp.int32, (1, 2 * NA_HEAD_DIM), 1) < NA_HEAD_DIM

    def split_heads(qv):
        zero = jnp.zeros_like(qv)
        return jnp.concatenate([jnp.where(first, qv, zero), jnp.where(first, zero, qv)], axis=0)

    def merge_heads(o, n):
        return jnp.where(first, o[:n], o[n:])

    kc = k_ref[0, 0:n_ctx, :]
    vc = v_ref[0, 0:n_ctx, :]

    qs = split_heads(q_ref[0, 0:n_ctx, :])
    s = _dot_nt(qs, kc) * scale
    p = jnp.exp(s - jnp.max(s, axis=-1, keepdims=True))
    o = _dot(p.astype(BF16), vc) / jnp.sum(p, axis=-1, keepdims=True)
    y_ref[0, 0:n_ctx, :] = merge_heads(o, n_ctx).astype(y_ref.dtype)

    def row(r, carry):
        rs = jnp.clip(r - kr // 2, 0, rows - kr)
        case = r - rs
        q0 = pl.multiple_of(n_ctx + r * w, w)
        k0 = pl.multiple_of(n_ctx + rs * w, w)
        qs = split_heads(q_ref[0, pl.ds(q0, w), :])
        kl = k_ref[0, pl.ds(k0, kr * w), :]
        vl = v_ref[0, pl.ds(k0, kr * w), :]
        bias = jnp.concatenate([bias_ref[0, case], bias_ref[1, case]], axis=0)
        s_loc = _dot_nt(qs, kl) * scale + bias
        s_ctx = _dot_nt(qs, kc) * scale
        m = jnp.maximum(jnp.max(s_loc, axis=-1, keepdims=True), jnp.max(s_ctx, axis=-1, keepdims=True))
        p_loc = jnp.exp(s_loc - m)
        p_ctx = jnp.exp(s_ctx - m)
        den = jnp.sum(p_loc, axis=-1, keepdims=True) + jnp.sum(p_ctx, axis=-1, keepdims=True)
        o = (_dot(p_loc.astype(BF16), vl) + _dot(p_ctx.astype(BF16), vc)) / den
        y_ref[0, pl.ds(q0, w), :] = merge_heads(o, w).astype(y_ref.dtype)
        return carry

    lax.fori_loop(0, rows, row, 0)


def _na_attention(qkv, bias, *, n_ctx):
    b, t, n = qkv.shape
    d = n // 3
    lanes = 2 * NA_HEAD_DIM
    pairs = d // lanes
    rows = (t - n_ctx) // GRID_W
    kr = bias.shape[1]
    part = lambda p: pl.BlockSpec((1, t, lanes), lambda h, i: (i, 0, p * pairs + h))
    kern = functools.partial(_na_kernel, n_ctx=n_ctx, rows=rows)
    return pl.pallas_call(
        kern,
        grid=(pairs, b),
        in_specs=[
            part(0), part(1), part(2),
            pl.BlockSpec((2, kr, GRID_W, kr * GRID_W), lambda h, i: (h, 0, 0, 0)),
        ],
        out_specs=pl.BlockSpec((1, t, lanes), lambda h, i: (i, 0, h)),
        out_shape=jax.ShapeDtypeStruct((b, t, d), BF16),
        compiler_params=pltpu.CompilerParams(
            dimension_semantics=("arbitrary", "arbitrary"), vmem_limit_bytes=VMEM_LIMIT),
        name="na_attention",
    )(qkv, qkv, qkv, bias)


def kernel(x, c, ctx, c_ctx, mod_w, mod_b, ln_g, ln_b, ffn_w_in, ffn_w_out, pool_w, pool_scale,
           hgrn_w_in, hgrn_lb_logits, hgrn_norm_w, hgrn_w_out, na_w_qkv, na_rpb, na_w_out):
    b, seq, d = x.shape
    n_ctx = ctx.shape[1]
    depth = mod_w.shape[0]
    assert n_ctx % TOKEN_TILE == 0 and seq % TOKEN_TILE == 0 and b < MOD_ROWS
    assert seq % GRID_W == 0 and n_ctx % HGRN_CHUNK == 0 and seq % HGRN_CHUNK == 0

    cc = jnp.concatenate([c, c_ctx[None, :], jnp.zeros((MOD_ROWS - b - 1, d), c.dtype)], axis=0)
    mod_all = _modulation(cc, mod_w, mod_b)
    xx = jnp.concatenate([ctx, x], axis=1)

    for i in range(depth):
        last = i == depth - 1
        kind = i % N_MIXERS
        j = i // N_MIXERS
        mod6 = mod_all[i].reshape(MOD_ROWS, N_MOD, d)
        w_in = ffn_w_in[i].astype(BF16)
        w_out = ffn_w_out[i].astype(BF16)
        if kind == 0:
            y = _pool_mixer(xx, mod6, pool_w[j], pool_scale[j], n_ctx)
            w_mix = None
        elif kind == 1:
            proj = _modulated_proj(xx, mod6, hgrn_w_in[j].astype(BF16), F32, n_chunk=d)
            y = _hgrn_scan(proj, hgrn_lb_logits, hgrn_norm_w[j], layer=i, n_ctx=n_ctx)
            w_mix = hgrn_w_out[j].astype(BF16)
        else:
            qkv = _modulated_proj(xx, mod6, na_w_qkv[j].astype(BF16), BF16, n_chunk=d)
            bias = _na_bias_table(na_rpb[j], seq // GRID_W)
            y = _na_attention(qkv, bias, n_ctx=n_ctx)
            w_mix = na_w_out[j].astype(BF16)
        xx = _ffn_block(xx, y, mod6, ln_g[i], ln_b[i], w_mix, w_in, w_out,
                        latent_only=last, n_ctx=n_ctx)
    return xx
```

```python
import functools

import numpy as np
import jax
import jax.numpy as jnp
from jax import lax
from jax.experimental import pallas as pl
from jax.experimental.pallas import tpu as pltpu

DEPTH = 4
N_MIXERS = 3
N_MOD = 6
ALPHA = (2 * DEPTH) ** 0.25
LN_EPS = 1e-6
RMS_EPS = 1e-6
POOL_WINDOWS = (2, 4, 8, 16)
HGRN_HEAD_DIM = 128
HGRN_CHUNK = 64
NA_HEAD_DIM = 64
NA_ROWS = 8
NA_COLS = 16
GRID_W = 64
NEG_BIG = -1e30

TOKEN_TILE = 256
MOD_ROWS = 16
VMEM_LIMIT = 56 * 1024 * 1024

F32 = jnp.float32
BF16 = jnp.bfloat16


def _sigmoid(x):
    return 1.0 / (1.0 + jnp.exp(-x))


def _silu(x):
    return x * _sigmoid(x)


def _layer_norm(v, g, b):
    mu = jnp.mean(v, axis=-1, keepdims=True)
    d = v - mu
    var = jnp.mean(d * d, axis=-1, keepdims=True)
    return d * lax.rsqrt(var + LN_EPS) * g + b


def _dot(a, b):
    return jnp.dot(a, b, preferred_element_type=F32)


def _dot_nt(a, b):
    return lax.dot_general(a, b, (((1,), (1,)), ((), ())), preferred_element_type=F32)


def _dot_tn(a, b):
    return lax.dot_general(a, b, (((0,), (0,)), ((), ())), preferred_element_type=F32)


def _resident(shape):
    zeros = (0,) * len(shape)
    return pl.BlockSpec(shape, lambda *_: zeros, pipeline_mode=pl.Buffered(1))


def _mod_kernel(c_ref, w_ref, b_ref, o_ref):
    s = _silu(c_ref[...]).astype(BF16)
    o_ref[0] = _dot(s, w_ref[0].astype(BF16)) + b_ref[0]


def _modulation(cc, mod_w, mod_b):
    depth, d, n = mod_w.shape
    tn = n // 4
    return pl.pallas_call(
        _mod_kernel,
        grid=(depth, n // tn),
        in_specs=[
            pl.BlockSpec((MOD_ROWS, d), lambda i, j: (0, 0)),
            pl.BlockSpec((1, d, tn), lambda i, j: (i, 0, j)),
            pl.BlockSpec((1, 1, tn), lambda i, j: (i, 0, j)),
        ],
        out_specs=pl.BlockSpec((1, MOD_ROWS, tn), lambda i, j: (i, 0, j)),
        out_shape=jax.ShapeDtypeStruct((depth, MOD_ROWS, n), F32),
        compiler_params=pltpu.CompilerParams(
            dimension_semantics=("arbitrary", "arbitrary"), vmem_limit_bytes=VMEM_LIMIT),
        name="modulation",
    )(cc, mod_w, mod_b.reshape(depth, 1, n))


def _pool_kernel(x_ref, modx_ref, modc_ref, w_ref, s_ref, o_ref, pad_ref, *, n_ctx, halo):
    g = pl.program_id(1)
    t_all = x_ref.shape[1]
    wg = w_ref[0].astype(BF16)
    scale = s_ref[...]

    def segment(r0, length, mod_ref, window):
        lo = window // 2
        hi = window - 1 - lo
        h = x_ref[0, r0:r0 + length, :] * (1.0 + mod_ref[0, 1:2, :]) + mod_ref[0, 0:1, :]
        zeros = jnp.zeros((halo, h.shape[1]), F32)
        pad_ref[0:halo, :] = zeros
        pad_ref[halo:halo + length, :] = h
        pad_ref[halo + length:2 * halo + length, :] = zeros
        tot = h
        for off in range(-lo, hi + 1):
            if off != 0:
                tot = tot + pad_ref[halo + off:halo + off + length, :]
        t = lax.broadcasted_iota(jnp.int32, h.shape, 0)
        cnt = jnp.minimum(t + hi + 1, length) - jnp.maximum(t - lo, 0)
        p = tot / cnt.astype(F32) - h
        o_ref[0, r0:r0 + length, :] = _dot(p.astype(BF16), wg) * scale

    for gi, window in enumerate(POOL_WINDOWS):
        @pl.when(g == gi)
        def _(window=window):
            segment(0, n_ctx, modc_ref, window)
            segment(n_ctx, t_all - n_ctx, modx_ref, window)


def _pool_mixer(xx, mod6, w_group, scale, n_ctx):
    b, t, d = xx.shape
    ng, gd, _ = w_group.shape
    halo = 8
    kern = functools.partial(_pool_kernel, n_ctx=n_ctx, halo=halo)
    return pl.pallas_call(
        kern,
        grid=(b, ng),
        in_specs=[
            pl.BlockSpec((1, t, gd), lambda i, g: (i, 0, g)),
            pl.BlockSpec((1, N_MOD, gd), lambda i, g: (i, 0, g)),
            pl.BlockSpec((1, N_MOD, gd), lambda i, g: (b, 0, g)),
            pl.BlockSpec((1, gd, gd), lambda i, g: (g, 0, 0)),
            pl.BlockSpec((1, gd), lambda i, g: (0, g)),
        ],
        out_specs=pl.BlockSpec((1, t, gd), lambda i, g: (i, 0, g)),
        out_shape=jax.ShapeDtypeStruct((b, t, d), F32),
        scratch_shapes=[pltpu.VMEM((t - n_ctx + 2 * halo, gd), F32)],
        compiler_params=pltpu.CompilerParams(
            dimension_semantics=("arbitrary", "arbitrary"), vmem_limit_bytes=VMEM_LIMIT),
        name="pool_mixer",
    )(xx, mod6, mod6, w_group, scale.reshape(1, d))


def _proj_kernel(x_ref, mod_ref, w_ref, o_ref, *, n_chunk):
    h = (x_ref[0] * (1.0 + mod_ref[0, 1:2, :]) + mod_ref[0, 0:1, :]).astype(BF16)
    n = w_ref.shape[1]
    for c0 in range(0, n, n_chunk):
        o_ref[0, :, c0:c0 + n_chunk] = _dot(h, w_ref[:, c0:c0 + n_chunk]).astype(o_ref.dtype)


def _mod_row_map(n_batch):
    return lambda i, j: (jnp.where(j > 0, i, n_batch), 0, 0)


def _modulated_proj(xx, mod6, w, out_dtype, n_chunk):
    b, t, d = xx.shape
    n = w.shape[1]
    kern = functools.partial(_proj_kernel, n_chunk=n_chunk)
    return pl.pallas_call(
        kern,
        grid=(b, t // TOKEN_TILE),
        in_specs=[
            pl.BlockSpec((1, TOKEN_TILE, d), lambda i, j: (i, j, 0)),
            pl.BlockSpec((1, N_MOD, d), _mod_row_map(b)),
            _resident((d, n)),
        ],
        out_specs=pl.BlockSpec((1, TOKEN_TILE, n), lambda i, j: (i, j, 0)),
        out_shape=jax.ShapeDtypeStruct((b, t, n), out_dtype),
        compiler_params=pltpu.CompilerParams(
            dimension_semantics=("arbitrary", "arbitrary"), vmem_limit_bytes=VMEM_LIMIT),
        name="modulated_proj",
    )(xx, mod6, w)


def _ffn_kernel(*refs, has_mix, ffn_hidden):
    if has_mix:
        x_ref, y_ref, mod_ref, lng_ref, lnb_ref, wmix_ref, win_ref, wout_ref, o_ref = refs
    else:
        x_ref, y_ref, mod_ref, lng_ref, lnb_ref, win_ref, wout_ref, o_ref = refs
    x = x_ref[0]
    g1 = mod_ref[0, 2:3, :]
    sh2 = mod_ref[0, 3:4, :]
    sc2 = mod_ref[0, 4:5, :]
    g2 = mod_ref[0, 5:6, :]
    if has_mix:
        ox = _dot(y_ref[0], wmix_ref[...])
    else:
        ox = y_ref[0]
    x1 = _layer_norm(ALPHA * x + g1 * ox, lng_ref[0:1, :], lnb_ref[0:1, :])
    h2 = (x1 * (1.0 + sc2) + sh2).astype(BF16)
    gu = _dot(h2, win_ref[...])
    act = (_silu(gu[:, :ffn_hidden]) * gu[:, ffn_hidden:]).astype(BF16)
    y2 = _dot(act, wout_ref[...])
    o_ref[0] = _layer_norm(ALPHA * x1 + g2 * y2, lng_ref[1:2, :], lnb_ref[1:2, :])


def _ffn_block(xx, y, mod6, ln_g, ln_b, w_mix, w_in, w_out, *, latent_only, n_ctx):
    b, t, d = xx.shape
    f = w_out.shape[0]
    off = n_ctx // TOKEN_TILE if latent_only else 0
    n_tiles = t // TOKEN_TILE - off
    has_mix = w_mix is not None
    tok = lambda i, j: (i, j + off, 0)
    in_specs = [
        pl.BlockSpec((1, TOKEN_TILE, d), tok),
        pl.BlockSpec((1, TOKEN_TILE, d), tok),
        pl.BlockSpec((1, N_MOD, d), lambda i, j: (jnp.where(j + off > 0, i, b), 0, 0)),
        _resident((2, d)),
        _resident((2, d)),
    ]
    args = [xx, y, mod6, ln_g, ln_b]
    if has_mix:
        in_specs.append(_resident((d, d)))
        args.append(w_mix)
    in_specs += [_resident((d, 2 * f)), _resident((f, d))]
    args += [w_in, w_out]
    kern = functools.partial(_ffn_kernel, has_mix=has_mix, ffn_hidden=f)
    return pl.pallas_call(
        kern,
        grid=(b, n_tiles),
        in_specs=in_specs,
        out_specs=pl.BlockSpec((1, TOKEN_TILE, d), lambda i, j: (i, j, 0)),
        out_shape=jax.ShapeDtypeStruct((b, n_tiles * TOKEN_TILE, d), F32),
        compiler_params=pltpu.CompilerParams(
            dimension_semantics=("arbitrary", "arbitrary"), vmem_limit_bytes=VMEM_LIMIT),
        name="ffn_block",
    )(*args)


def _hgrn_constants():
    c = HGRN_CHUNK
    t = np.arange(c)[:, None]
    u = np.arange(c)[None, :]
    mats = [u <= t]
    masks = [np.broadcast_to(t == u, (c, c))]
    m = c // 2
    while m >= 1:
        ref = (t // (2 * m)) * 2 * m + m - 1
        second = (t % (2 * m)) >= m
        mats.append(np.where(second, (u > ref) & (u <= t), (u > t) & (u <= ref)))
        masks.append(((t // (2 * m)) == (u // (2 * m))) & second & ((u % (2 * m)) < m))
        m //= 2
    mats.append(u > t)
    masks.append(np.zeros((c, c), bool))
    pair = lambda ms: np.stack([np.concatenate([ms[2 * p], ms[2 * p + 1]], 1) for p in range(len(ms) // 2)])
    flip = lambda ms: [a[::-1, ::-1] for a in ms]
    dmat = np.stack([np.concatenate(mats, 0), np.concatenate(flip(mats), 0)])
    dmat = np.concatenate([dmat] * 3, axis=2)
    mask = np.stack([pair(masks), pair(flip(masks))])
    return dmat.astype(np.float32), mask.astype(np.float32)


def _hgrn_scan_kernel(q_ref, v_ref, zf_ref, zb_ref, g_ref, lbl_ref, nw_ref, dm_ref, mk_ref,
                      y_ref, o_ref, qx_ref, u_ref, dec_ref, st_ref, *, layer, n_ctx):
    c = HGRN_CHUNK
    hd = HGRN_HEAD_DIM
    t_all = q_ref.shape[1]
    n_chunks = t_all // c
    n_ctx_chunks = n_ctx // c
    n_pairs = mk_ref.shape[1]
    n_blocks = dm_ref.shape[1] // c

    def lower_bound(direction):
        rows = [lbl_ref[direction, j:j + 1, :] for j in range(lbl_ref.shape[1])]
        mx = functools.reduce(jnp.maximum, rows)
        e = [jnp.exp(r - mx) for r in rows]
        tot = functools.reduce(lambda a, b_: a + b_, e)
        p = [ei / tot for ei in e]
        acc = p[0]
        for j in range(1, layer + 1):
            acc = acc + p[j]
        return acc - p[0]

    lb = (lower_bound(0), lower_bound(1))
    z_refs = (zf_ref, zb_ref)
    end_row = (c - 1, 0)
    zeros_k = jnp.zeros((c, hd), BF16)

    def intra(i, carry):
        r0 = pl.multiple_of(i * 2 * c, 2 * c)
        q = _silu(q_ref[0, pl.ds(r0, 2 * c), :])
        vb = v_ref[0, pl.ds(r0, 2 * c), :].astype(BF16)
        for d in range(2):
            f = lb[d] + (1.0 - lb[d]) * _sigmoid(z_refs[d][0, pl.ds(r0, 2 * c), :])
            k = 1.0 - f
            g = jnp.log(f)
            hi = g.astype(BF16)
            r1 = g - hi.astype(F32)
            mid = r1.astype(BF16)
            lo = (r1 - mid.astype(F32)).astype(BF16)
            side = lambda a: jnp.concatenate([a[:c], a[c:]], axis=1)
            x2 = jnp.exp(_dot(dm_ref[d], jnp.concatenate([side(hi), side(mid), side(lo)], axis=0)))
            for cc in range(2):
                xs = x2[:, cc * hd:(cc + 1) * hd]
                qc = q[cc * c:(cc + 1) * c]
                kc = k[cc * c:(cc + 1) * c]
                vc = vb[cc * c:(cc + 1) * c]
                lvl_q = [qc] + [qc * xs[l * c:(l + 1) * c] for l in range(1, n_blocks - 1)]
                lvl_k = [kc] + [kc * xs[l * c:(l + 1) * c] for l in range(1, n_blocks - 1)]
                r = None
                for p in range(n_pairs):
                    la, lb_ = 2 * p, 2 * p + 1
                    qa = lvl_q[la].astype(BF16)
                    ka = lvl_k[la].astype(BF16)
                    if lb_ < len(lvl_q):
                        qb = lvl_q[lb_].astype(BF16)
                        kb = lvl_k[lb_].astype(BF16)
                    else:
                        qb = kb = zeros_k
                    lhs = jnp.concatenate([qa, qb], axis=1)
                    rhs = jnp.concatenate([jnp.concatenate([ka, zeros_k], axis=1),
                                           jnp.concatenate([zeros_k, kb], axis=1)], axis=0)
                    rp = mk_ref[d, p] * _dot_nt(lhs, rhs)
                    r = rp if r is None else r + rp
                rows = pl.ds(r0 + cc * c, c)
                o_ref[d, rows, :] = _dot(r.astype(BF16), jnp.concatenate([vc, vc], axis=0))
                qx_ref[d, rows, :] = (qc * xs[0:c]).astype(BF16)
                k_end = (kc * xs[(n_blocks - 1) * c:n_blocks * c]).astype(BF16)
                u_ref[d, 2 * i + cc] = _dot_tn(vc, k_end)
                dec_ref[d, 2 * i + cc] = jnp.broadcast_to(xs[end_row[d]:end_row[d] + 1], (8, hd))
        return carry

    lax.fori_loop(0, n_chunks // 2, intra, 0)

    st_ref[...] = jnp.zeros(st_ref.shape, F32)

    def inter(i, carry):
        cb = jnp.where(i < n_ctx_chunks, n_ctx_chunks - 1 - i, n_chunks - 1 - (i - n_ctx_chunks))
        for d, ci in ((0, i), (1, cb)):
            rows = pl.ds(pl.multiple_of(ci * c, c), c)
            st = st_ref[d]
            o_ref[d, rows, :] += _dot_nt(qx_ref[d, rows, :], st.astype(BF16))
            st_ref[d] = st * dec_ref[d, ci][0:1] + u_ref[d, ci]
        return carry

    lax.fori_loop(0, n_chunks, inter, 0, unroll=4)

    nw = nw_ref[...]

    def finish(j, carry):
        rows = pl.ds(pl.multiple_of(j * TOKEN_TILE, TOKEN_TILE), TOKEN_TILE)
        o = o_ref[0, rows, :] + o_ref[1, rows, :]
        o = o * lax.rsqrt(jnp.mean(o * o, axis=-1, keepdims=True) + RMS_EPS)
        gate = _silu(g_ref[0, rows, :])
        y_ref[0, rows, :] = (o * nw * gate).astype(y_ref.dtype)
        return carry

    lax.fori_loop(0, t_all // TOKEN_TILE, finish, 0)


def _hgrn_scan(proj, lb_logits, norm_w, *, layer, n_ctx):
    b, t, n = proj.shape
    d = n // 5
    hd = HGRN_HEAD_DIM
    heads = d // hd
    n_chunks = t // HGRN_CHUNK
    assert n_chunks % 4 == 0
    dmat, mask = _hgrn_constants()
    part = lambda p: pl.BlockSpec((1, t, hd), lambda i, h: (i, 0, p * heads + h))
    kern = functools.partial(_hgrn_scan_kernel, layer=layer, n_ctx=n_ctx)
    return pl.pallas_call(
        kern,
        grid=(b, heads),
        in_specs=[
            part(0), part(1), part(2), part(3), part(4),
            pl.BlockSpec((2, lb_logits.shape[1], hd), lambda i, h: (0, 0, h)),
            pl.BlockSpec((1, hd), lambda i, h: (0, h)),
            _resident(dmat.shape),
            _resident(mask.shape),
        ],
        out_specs=pl.BlockSpec((1, t, hd), lambda i, h: (i, 0, h)),
        out_shape=jax.ShapeDtypeStruct((b, t, d), BF16),
        scratch_shapes=[
            pltpu.VMEM((2, t, hd), F32),
            pltpu.VMEM((2, t, hd), BF16),
            pltpu.VMEM((2, n_chunks, hd, hd), F32),
            pltpu.VMEM((2, n_chunks, 8, hd), F32),
            pltpu.VMEM((2, hd, hd), F32),
        ],
        compiler_params=pltpu.CompilerParams(
            dimension_semantics=("arbitrary", "arbitrary"), vmem_limit_bytes=VMEM_LIMIT),
        name="hgrn_scan",
    )(proj, proj, proj, proj, proj, lb_logits, norm_w.reshape(1, d),
      jnp.asarray(dmat, BF16), jnp.asarray(mask, F32))


def _na_bias_table(rpb, rows):
    kr = min(NA_ROWS, rows)
    qcol = np.arange(GRID_W)[:, None]
    kcol = np.arange(GRID_W)[None, :]
    win = np.clip(qcol - NA_COLS // 2, 0, GRID_W - NA_COLS)
    ok = (kcol >= win) & (kcol < win + NA_COLS)
    dcol = np.clip(kcol - qcol + NA_COLS - 1, 0, 2 * NA_COLS - 2)
    case = np.arange(kr)[:, None]
    a = np.arange(kr)[None, :]
    drow = np.clip(a - case + NA_ROWS - 1, 0, 2 * NA_ROWS - 2)
    bias = rpb[:, drow][:, :, :, dcol]
    bias = jnp.where(jnp.asarray(ok)[None, None, None], bias.astype(F32), NEG_BIG)
    bias = bias.transpose(0, 1, 3, 2, 4)
    return bias.reshape(rpb.shape[0], kr, GRID_W, kr * GRID_W)


def _na_kernel(q_ref, k_ref, v_ref, bias_ref, y_ref, *, n_ctx, rows):
    w = GRID_W
    kr = min(NA_ROWS, rows)
    scale = NA_HEAD_DIM ** -0.5
    first = lax.broadcasted_iota(jnp.int32, (1, 2 * NA_HEAD_DIM), 1) < NA_HEAD_DIM

    def split_heads(qv):
        zero = jnp.zeros_like(qv)
        return jnp.concatenate([jnp.where(first, qv, zero), jnp.where(first, zero, qv)], axis=0)

    def merge_heads(o, n):
        return jnp.where(first, o[:n], o[n:])

    kc = k_ref[0, 0:n_ctx, :]
    vc = v_ref[0, 0:n_ctx, :]

    qs = split_heads(q_ref[0, 0:n_ctx, :])
    s = _dot_nt(qs, kc) * scale
    p = jnp.exp(s - jnp.max(s, axis=-1, keepdims=True))
    o = _dot(p.astype(BF16), vc) / jnp.sum(p, axis=-1, keepdims=True)
    y_ref[0, 0:n_ctx, :] = merge_heads(o, n_ctx).astype(y_ref.dtype)

    def row(r, carry):
        rs = jnp.clip(r - kr // 2, 0, rows - kr)
        case = r - rs
        q0 = pl.multiple_of(n_ctx + r * w, w)
        k0 = pl.multiple_of(n_ctx + rs * w, w)
        qs = split_heads(q_ref[0, pl.ds(q0, w), :])
        kl = k_ref[0, pl.ds(k0, kr * w), :]
        vl = v_ref[0, pl.ds(k0, kr * w), :]
        bias = jnp.concatenate([bias_ref[0, case], bias_ref[1, case]], axis=0)
        s_loc = _dot_nt(qs, kl) * scale + bias
        s_ctx = _dot_nt(qs, kc) * scale
        m = jnp.maximum(jnp.max(s_loc, axis=-1, keepdims=True), jnp.max(s_ctx, axis=-1, keepdims=True))
        p_loc = jnp.exp(s_loc - m)
        p_ctx = jnp.exp(s_ctx - m)
        den = jnp.sum(p_loc, axis=-1, keepdims=True) + jnp.sum(p_ctx, axis=-1, keepdims=True)
        o = (_dot(p_loc.astype(BF16), vl) + _dot(p_ctx.astype(BF16), vc)) / den
        y_ref[0, pl.ds(q0, w), :] = merge_heads(o, w).astype(y_ref.dtype)
        return carry

    lax.fori_loop(0, rows, row, 0)


def _na_attention(qkv, bias, *, n_ctx):
    b, t, n = qkv.shape
    d = n // 3
    lanes = 2 * NA_HEAD_DIM
    pairs = d // lanes
    rows = (t - n_ctx) // GRID_W
    kr = bias.shape[1]
    part = lambda p: pl.BlockSpec((1, t, lanes), lambda h, i: (i, 0, p * pairs + h))
    kern = functools.partial(_na_kernel, n_ctx=n_ctx, rows=rows)
    return pl.pallas_call(
        kern,
        grid=(pairs, b),
        in_specs=[
            part(0), part(1), part(2),
            pl.BlockSpec((2, kr, GRID_W, kr * GRID_W), lambda h, i: (h, 0, 0, 0)),
        ],
        out_specs=pl.BlockSpec((1, t, lanes), lambda h, i: (i, 0, h)),
        out_shape=jax.ShapeDtypeStruct((b, t, d), BF16),
        compiler_params=pltpu.CompilerParams(
            dimension_semantics=("arbitrary", "arbitrary"), vmem_limit_bytes=VMEM_LIMIT),
        name="na_attention",
    )(qkv, qkv, qkv, bias)


def kernel(x, c, ctx, c_ctx, mod_w, mod_b, ln_g, ln_b, ffn_w_in, ffn_w_out, pool_w, pool_scale,
           hgrn_w_in, hgrn_lb_logits, hgrn_norm_w, hgrn_w_out, na_w_qkv, na_rpb, na_w_out):
    b, seq, d = x.shape
    n_ctx = ctx.shape[1]
    depth = mod_w.shape[0]
    assert n_ctx % TOKEN_TILE == 0 and seq % TOKEN_TILE == 0 and b < MOD_ROWS
    assert seq % GRID_W == 0 and n_ctx % HGRN_CHUNK == 0 and seq % HGRN_CHUNK == 0

    cc = jnp.concatenate([c, c_ctx[None, :], jnp.zeros((MOD_ROWS - b - 1, d), c.dtype)], axis=0)
    mod_all = _modulation(cc, mod_w, mod_b)
    xx = jnp.concatenate([ctx, x], axis=1)

    for i in range(depth):
        last = i == depth - 1
        kind = i % N_MIXERS
        j = i // N_MIXERS
        mod6 = mod_all[i].reshape(MOD_ROWS, N_MOD, d)
        w_in = ffn_w_in[i].astype(BF16)
        w_out = ffn_w_out[i].astype(BF16)
        if kind == 0:
            y = _pool_mixer(xx, mod6, pool_w[j], pool_scale[j], n_ctx)
            w_mix = None
        elif kind == 1:
            proj = _modulated_proj(xx, mod6, hgrn_w_in[j].astype(BF16), F32, n_chunk=d)
            y = _hgrn_scan(proj, hgrn_lb_logits, hgrn_norm_w[j], layer=i, n_ctx=n_ctx)
            w_mix = hgrn_w_out[j].astype(BF16)
        else:
            qkv = _modulated_proj(xx, mod6, na_w_qkv[j].astype(BF16), BF16, n_chunk=d)
            bias = _na_bias_table(na_rpb[j], seq // GRID_W)
            y = _na_attention(qkv, bias, n_ctx=n_ctx)
            w_mix = na_w_out[j].astype(BF16)
        xx = _ffn_block(xx, y, mod6, ln_g[i], ln_b[i], w_mix, w_in, w_out,
                        latent_only=last, n_ctx=n_ctx)
    return xx
```

```python
import functools

import numpy as np
import jax
import jax.numpy as jnp
from jax import lax
from jax.experimental import pallas as pl
from jax.experimental.pallas import tpu as pltpu

DEPTH = 4
N_MIXERS = 3
N_MOD = 6
ALPHA = (2 * DEPTH) ** 0.25
LN_EPS = 1e-6
RMS_EPS = 1e-6
POOL_WINDOWS = (2, 4, 8, 16)
HGRN_HEAD_DIM = 128
HGRN_CHUNK = 64
NA_HEAD_DIM = 64
NA_ROWS = 8
NA_COLS = 16
GRID_W = 64
NEG_BIG = -1e30
LOG2E = 1.4426950408889634
NA_ROW_GROUP = 4

TOKEN_TILE = 256
MOD_ROWS = 16
VMEM_LIMIT = 56 * 1024 * 1024

F32 = jnp.float32
BF16 = jnp.bfloat16


def _sigmoid(x):
    return 1.0 / (1.0 + jnp.exp(-x))


def _silu(x):
    return x * _sigmoid(x)


def _layer_norm(v, g, b):
    mu = jnp.mean(v, axis=-1, keepdims=True)
    d = v - mu
    var = jnp.mean(d * d, axis=-1, keepdims=True)
    return d * lax.rsqrt(var + LN_EPS) * g + b


def _dot(a, b):
    return jnp.dot(a, b, preferred_element_type=F32)


def _dot_nt(a, b):
    return lax.dot_general(a, b, (((1,), (1,)), ((), ())), preferred_element_type=F32)


def _dot_tn(a, b):
    return lax.dot_general(a, b, (((0,), (0,)), ((), ())), preferred_element_type=F32)


def _resident(shape):
    zeros = (0,) * len(shape)
    return pl.BlockSpec(shape, lambda *_: zeros, pipeline_mode=pl.Buffered(1))


def _mod_kernel(c_ref, w_ref, b_ref, o_ref):
    s = _silu(c_ref[...]).astype(BF16)
    o_ref[0] = _dot(s, w_ref[0].astype(BF16)) + b_ref[0]


def _modulation(cc, mod_w, mod_b):
    depth, d, n = mod_w.shape
    tn = n // 4
    return pl.pallas_call(
        _mod_kernel,
        grid=(depth, n // tn),
        in_specs=[
            pl.BlockSpec((MOD_ROWS, d), lambda i, j: (0, 0)),
            pl.BlockSpec((1, d, tn), lambda i, j: (i, 0, j)),
            pl.BlockSpec((1, 1, tn), lambda i, j: (i, 0, j)),
        ],
        out_specs=pl.BlockSpec((1, MOD_ROWS, tn), lambda i, j: (i, 0, j)),
        out_shape=jax.ShapeDtypeStruct((depth, MOD_ROWS, n), F32),
        compiler_params=pltpu.CompilerParams(
            dimension_semantics=("arbitrary", "arbitrary"), vmem_limit_bytes=VMEM_LIMIT),
        name="modulation",
    )(cc, mod_w, mod_b.reshape(depth, 1, n))


def _pool_kernel(x_ref, modx_ref, modc_ref, w_ref, s_ref, o_ref, pad_ref, *, n_ctx, halo):
    g = pl.program_id(1)
    t_all = x_ref.shape[1]
    wg = w_ref[0].astype(BF16)
    scale = s_ref[...]

    def segment(r0, length, mod_ref, window):
        lo = window // 2
        hi = window - 1 - lo
        h = x_ref[0, r0:r0 + length, :] * (1.0 + mod_ref[0, 1:2, :]) + mod_ref[0, 0:1, :]
        zeros = jnp.zeros((halo, h.shape[1]), F32)
        pad_ref[0:halo, :] = zeros
        pad_ref[halo:halo + length, :] = h
        pad_ref[halo + length:2 * halo + length, :] = zeros
        tot = h
        for off in range(-lo, hi + 1):
            if off != 0:
                tot = tot + pad_ref[halo + off:halo + off + length, :]
        t = lax.broadcasted_iota(jnp.int32, h.shape, 0)
        cnt = jnp.minimum(t + hi + 1, length) - jnp.maximum(t - lo, 0)
        p = tot / cnt.astype(F32) - h
        o_ref[0, r0:r0 + length, :] = _dot(p.astype(BF16), wg) * scale

    for gi, window in enumerate(POOL_WINDOWS):
        @pl.when(g == gi)
        def _(window=window):
            segment(0, n_ctx, modc_ref, window)
            segment(n_ctx, t_all - n_ctx, modx_ref, window)


def _pool_mixer(xx, mod6, w_group, scale, n_ctx):
    b, t, d = xx.shape
    ng, gd, _ = w_group.shape
    halo = 8
    kern = functools.partial(_pool_kernel, n_ctx=n_ctx, halo=halo)
    return pl.pallas_call(
        kern,
        grid=(b, ng),
        in_specs=[
            pl.BlockSpec((1, t, gd), lambda i, g: (i, 0, g)),
            pl.BlockSpec((1, N_MOD, gd), lambda i, g: (i, 0, g)),
            pl.BlockSpec((1, N_MOD, gd), lambda i, g: (b, 0, g)),
            pl.BlockSpec((1, gd, gd), lambda i, g: (g, 0, 0)),
            pl.BlockSpec((1, gd), lambda i, g: (0, g)),
        ],
        out_specs=pl.BlockSpec((1, t, gd), lambda i, g: (i, 0, g)),
        out_shape=jax.ShapeDtypeStruct((b, t, d), F32),
        scratch_shapes=[pltpu.VMEM((t - n_ctx + 2 * halo, gd), F32)],
        compiler_params=pltpu.CompilerParams(
            dimension_semantics=("arbitrary", "arbitrary"), vmem_limit_bytes=VMEM_LIMIT),
        name="pool_mixer",
    )(xx, mod6, mod6, w_group, scale.reshape(1, d))


def _proj_kernel(x_ref, mod_ref, w_ref, o_ref, *, n_chunk, first_chunk_scale):
    h = (x_ref[0] * (1.0 + mod_ref[0, 1:2, :]) + mod_ref[0, 0:1, :]).astype(BF16)
    n = w_ref.shape[1]
    for c0 in range(0, n, n_chunk):
        r = _dot(h, w_ref[:, c0:c0 + n_chunk])
        if c0 == 0 and first_chunk_scale != 1.0:
            r = r * first_chunk_scale
        o_ref[0, :, c0:c0 + n_chunk] = r.astype(o_ref.dtype)


def _mod_row_map(n_batch):
    return lambda i, j: (jnp.where(j > 0, i, n_batch), 0, 0)


def _modulated_proj(xx, mod6, w, out_dtype, n_chunk, first_chunk_scale=1.0):
    b, t, d = xx.shape
    n = w.shape[1]
    kern = functools.partial(_proj_kernel, n_chunk=n_chunk, first_chunk_scale=first_chunk_scale)
    return pl.pallas_call(
        kern,
        grid=(b, t // TOKEN_TILE),
        in_specs=[
            pl.BlockSpec((1, TOKEN_TILE, d), lambda i, j: (i, j, 0)),
            pl.BlockSpec((1, N_MOD, d), _mod_row_map(b)),
            _resident((d, n)),
        ],
        out_specs=pl.BlockSpec((1, TOKEN_TILE, n), lambda i, j: (i, j, 0)),
        out_shape=jax.ShapeDtypeStruct((b, t, n), out_dtype),
        compiler_params=pltpu.CompilerParams(
            dimension_semantics=("arbitrary", "arbitrary"), vmem_limit_bytes=VMEM_LIMIT),
        name="modulated_proj",
    )(xx, mod6, w)


def _ffn_kernel(*refs, has_mix, ffn_hidden):
    if has_mix:
        x_ref, y_ref, mod_ref, lng_ref, lnb_ref, wmix_ref, win_ref, wout_ref, o_ref = refs
    else:
        x_ref, y_ref, mod_ref, lng_ref, lnb_ref, win_ref, wout_ref, o_ref = refs
    x = x_ref[0]
    g1 = mod_ref[0, 2:3, :]
    sh2 = mod_ref[0, 3:4, :]
    sc2 = mod_ref[0, 4:5, :]
    g2 = mod_ref[0, 5:6, :]
    if has_mix:
        ox = _dot(y_ref[0], wmix_ref[...])
    else:
        ox = y_ref[0]
    x1 = _layer_norm(ALPHA * x + g1 * ox, lng_ref[0:1, :], lnb_ref[0:1, :])
    h2 = (x1 * (1.0 + sc2) + sh2).astype(BF16)
    gu = _dot(h2, win_ref[...])
    act = (_silu(gu[:, :ffn_hidden]) * gu[:, ffn_hidden:]).astype(BF16)
    y2 = _dot(act, wout_ref[...])
    o_ref[0] = _layer_norm(ALPHA * x1 + g2 * y2, lng_ref[1:2, :], lnb_ref[1:2, :])


def _ffn_block(xx, y, mod6, ln_g, ln_b, w_mix, w_in, w_out, *, latent_only, n_ctx):
    b, t, d = xx.shape
    f = w_out.shape[0]
    off = n_ctx // TOKEN_TILE if latent_only else 0
    n_tiles = t // TOKEN_TILE - off
    has_mix = w_mix is not None
    tok = lambda i, j: (i, j + off, 0)
    in_specs = [
        pl.BlockSpec((1, TOKEN_TILE, d), tok),
        pl.BlockSpec((1, TOKEN_TILE, d), tok),
        pl.BlockSpec((1, N_MOD, d), lambda i, j: (jnp.where(j + off > 0, i, b), 0, 0)),
        _resident((2, d)),
        _resident((2, d)),
    ]
    args = [xx, y, mod6, ln_g, ln_b]
    if has_mix:
        in_specs.append(_resident((d, d)))
        args.append(w_mix)
    in_specs += [_resident((d, 2 * f)), _resident((f, d))]
    args += [w_in, w_out]
    kern = functools.partial(_ffn_kernel, has_mix=has_mix, ffn_hidden=f)
    return pl.pallas_call(
        kern,
        grid=(b, n_tiles),
        in_specs=in_specs,
        out_specs=pl.BlockSpec((1, TOKEN_TILE, d), lambda i, j: (i, j, 0)),
        out_shape=jax.ShapeDtypeStruct((b, n_tiles * TOKEN_TILE, d), F32),
        compiler_params=pltpu.CompilerParams(
            dimension_semantics=("arbitrary", "arbitrary"), vmem_limit_bytes=VMEM_LIMIT),
        name="ffn_block",
    )(*args)


def _hgrn_constants():
    c = HGRN_CHUNK
    t = np.arange(c)[:, None]
    u = np.arange(c)[None, :]
    mats = [u <= t]
    masks = [np.broadcast_to(t == u, (c, c))]
    m = c // 2
    while m >= 1:
        ref = (t // (2 * m)) * 2 * m + m - 1
        second = (t % (2 * m)) >= m
        mats.append(np.where(second, (u > ref) & (u <= t), (u > t) & (u <= ref)))
        masks.append(((t // (2 * m)) == (u // (2 * m))) & second & ((u % (2 * m)) < m))
        m //= 2
    mats.append(u > t)
    masks.append(np.zeros((c, c), bool))
    pair = lambda ms: np.stack([np.concatenate([ms[2 * p], ms[2 * p + 1]], 1) for p in range(len(ms) // 2)])
    flip = lambda ms: [a[::-1, ::-1] for a in ms]
    dmat = np.stack([np.concatenate(mats, 0), np.concatenate(flip(mats), 0)])
    dmat = np.concatenate([dmat] * 3, axis=2)
    mask = np.stack([pair(masks), pair(flip(masks))])
    return dmat.astype(np.float32), mask.astype(np.float32)


def _hgrn_scan_kernel(q_ref, v_ref, zf_ref, zb_ref, g_ref, lbl_ref, nw_ref, dm_ref, mk_ref,
                      y_ref, o_ref, qx_ref, ke_ref, r_ref, qf_ref, kf_ref, x_ref, u_ref, dec_ref, st_ref,
                      *, layer, n_ctx):
    c = HGRN_CHUNK
    hd = HGRN_HEAD_DIM
    t_all = q_ref.shape[1]
    n_chunks = t_all // c
    n_ctx_chunks = n_ctx // c
    n_pairs = mk_ref.shape[1]
    n_blocks = dm_ref.shape[1] // c

    def lower_bound(direction):
        rows = [lbl_ref[direction, j:j + 1, :] for j in range(lbl_ref.shape[1])]
        mx = functools.reduce(jnp.maximum, rows)
        e = [jnp.exp(r - mx) for r in rows]
        tot = functools.reduce(lambda a, b_: a + b_, e)
        p = [ei / tot for ei in e]
        acc = p[0]
        for j in range(1, layer + 1):
            acc = acc + p[j]
        return acc - p[0]

    lb = (lower_bound(0), lower_bound(1))
    z_refs = (zf_ref, zb_ref)
    end_row = (c - 1, 0)
    zeros_k = jnp.zeros((c, hd), BF16)

    def decays(i, carry):
        rows2 = pl.ds(pl.multiple_of(i * 2 * c, 2 * c), 2 * c)
        q = _silu(q_ref[0, rows2, :])
        qf_ref[rows2, :] = q
        for d in range(2):
            f = lb[d] + (1.0 - lb[d]) * _sigmoid(z_refs[d][0, rows2, :])
            k = 1.0 - f
            kf_ref[d, rows2, :] = k
            g = jnp.log(f) * LOG2E
            hi = g.astype(BF16)
            r1 = g - hi.astype(F32)
            mid = r1.astype(BF16)
            lo = (r1 - mid.astype(F32)).astype(BF16)
            side = lambda a: jnp.concatenate([a[:c], a[c:]], axis=1)
            x2 = jnp.exp2(_dot(dm_ref[d], jnp.concatenate([side(hi), side(mid), side(lo)], axis=0)))
            for cc in range(2):
                xs = x2[:, cc * hd:(cc + 1) * hd]
                rows = pl.ds(pl.multiple_of(i * 2 * c + cc * c, c), c)
                x_ref[d, 2 * i + cc] = xs[c:(n_blocks - 1) * c]
                qx_ref[d, rows, :] = (q[cc * c:(cc + 1) * c] * xs[0:c]).astype(BF16)
                ke_ref[d, rows, :] = (k[cc * c:(cc + 1) * c] * xs[(n_blocks - 1) * c:]).astype(BF16)
                dec_ref[d, 2 * i + cc] = jnp.broadcast_to(xs[end_row[d]:end_row[d] + 1], (8, hd))
        return carry

    def interactions(ci, carry):
        rows = pl.ds(pl.multiple_of(ci * c, c), c)
        qc = qf_ref[rows, :]
        for d in range(2):
            kc = kf_ref[d, rows, :]
            xs = x_ref[d, ci]
            lvl_q = [qc] + [qc * xs[l * c:(l + 1) * c] for l in range(n_blocks - 2)]
            lvl_k = [kc] + [kc * xs[l * c:(l + 1) * c] for l in range(n_blocks - 2)]
            r = None
            for p in range(n_pairs):
                la, lb_ = 2 * p, 2 * p + 1
                qa = lvl_q[la].astype(BF16)
                ka = lvl_k[la].astype(BF16)
                if lb_ < len(lvl_q):
                    qb = lvl_q[lb_].astype(BF16)
                    kb = lvl_k[lb_].astype(BF16)
                else:
                    qb = kb = zeros_k
                lhs = jnp.concatenate([qa, qb], axis=1)
                rhs = jnp.concatenate([jnp.concatenate([ka, zeros_k], axis=1),
                                       jnp.concatenate([zeros_k, kb], axis=1)], axis=0)
                rp = mk_ref[d, p] * _dot_nt(lhs, rhs)
                r = rp if r is None else r + rp
            r_ref[d, rows, :] = r.astype(BF16)
        return carry

    def values(ci, carry):
        rows = pl.ds(pl.multiple_of(ci * c, c), c)
        vc = v_ref[0, rows, :].astype(BF16)
        oo = _dot(jnp.concatenate([r_ref[0, rows, :], r_ref[1, rows, :]], axis=0),
                  jnp.concatenate([vc, vc], axis=0))
        o_ref[0, rows, :] = oo[:c]
        o_ref[1, rows, :] = oo[c:]
        uu = _dot_tn(vc, jnp.concatenate([ke_ref[0, rows, :], ke_ref[1, rows, :]], axis=1))
        u_ref[0, ci] = uu[:, :hd]
        u_ref[1, ci] = uu[:, hd:]
        return carry

    lax.fori_loop(0, n_chunks // 2, decays, 0, unroll=3)
    lax.fori_loop(0, n_chunks, interactions, 0, unroll=4)
    lax.fori_loop(0, n_chunks, values, 0, unroll=6)

    st_ref[...] = jnp.zeros(st_ref.shape, F32)

    def inter(i, carry):
        cb = jnp.where(i < n_ctx_chunks, n_ctx_chunks - 1 - i, n_chunks - 1 - (i - n_ctx_chunks))
        for d, ci in ((0, i), (1, cb)):
            rows = pl.ds(pl.multiple_of(ci * c, c), c)
            st = st_ref[d]
            o_ref[d, rows, :] += _dot_nt(qx_ref[d, rows, :], st.astype(BF16))
            st_ref[d] = st * dec_ref[d, ci][0:1] + u_ref[d, ci]
        return carry

    lax.fori_loop(0, n_chunks, inter, 0, unroll=6)

    nw = nw_ref[...]

    def finish(j, carry):
        rows = pl.ds(pl.multiple_of(j * TOKEN_TILE, TOKEN_TILE), TOKEN_TILE)
        o = o_ref[0, rows, :] + o_ref[1, rows, :]
        o = o * lax.rsqrt(jnp.mean(o * o, axis=-1, keepdims=True) + RMS_EPS)
        gate = _silu(g_ref[0, rows, :])
        y_ref[0, rows, :] = (o * nw * gate).astype(y_ref.dtype)
        return carry

    lax.fori_loop(0, t_all // TOKEN_TILE, finish, 0)


def _hgrn_scan(proj, lb_logits, norm_w, *, layer, n_ctx):
    b, t, n = proj.shape
    d = n // 5
    hd = HGRN_HEAD_DIM
    heads = d // hd
    n_chunks = t // HGRN_CHUNK
    assert n_chunks % 4 == 0
    dmat, mask = _hgrn_constants()
    part = lambda p: pl.BlockSpec((1, t, hd), lambda i, h: (i, 0, p * heads + h))
    kern = functools.partial(_hgrn_scan_kernel, layer=layer, n_ctx=n_ctx)
    return pl.pallas_call(
        kern,
        grid=(b, heads),
        in_specs=[
            part(0), part(1), part(2), part(3), part(4),
            pl.BlockSpec((2, lb_logits.shape[1], hd), lambda i, h: (0, 0, h)),
            pl.BlockSpec((1, hd), lambda i, h: (0, h)),
            _resident(dmat.shape),
            _resident(mask.shape),
        ],
        out_specs=pl.BlockSpec((1, t, hd), lambda i, h: (i, 0, h)),
        out_shape=jax.ShapeDtypeStruct((b, t, d), BF16),
        scratch_shapes=[
            pltpu.VMEM((2, t, hd), F32),
            pltpu.VMEM((2, t, hd), BF16),
            pltpu.VMEM((2, t, hd), BF16),
            pltpu.VMEM((2, t, hd), BF16),
            pltpu.VMEM((t, hd), F32),
            pltpu.VMEM((2, t, hd), F32),
            pltpu.VMEM((2, n_chunks, (dmat.shape[1] // HGRN_CHUNK - 2) * HGRN_CHUNK, hd), F32),
            pltpu.VMEM((2, n_chunks, hd, hd), F32),
            pltpu.VMEM((2, n_chunks, 8, hd), F32),
            pltpu.VMEM((2, hd, hd), F32),
        ],
        compiler_params=pltpu.CompilerParams(
            dimension_semantics=("arbitrary", "arbitrary"), vmem_limit_bytes=VMEM_LIMIT),
        name="hgrn_scan",
    )(proj, proj, proj, proj, proj, lb_logits, norm_w.reshape(1, d),
      jnp.asarray(dmat, BF16), jnp.asarray(mask, F32))


def _na_bias_table(rpb, rows):
    kr = min(NA_ROWS, rows)
    qcol = np.arange(GRID_W)[:, None]
    kcol = np.arange(GRID_W)[None, :]
    win = np.clip(qcol - NA_COLS // 2, 0, GRID_W - NA_COLS)
    ok = (kcol >= win) & (kcol < win + NA_COLS)
    dcol = np.clip(kcol - qcol + NA_COLS - 1, 0, 2 * NA_COLS - 2)
    case = np.arange(kr)[:, None]
    a = np.arange(kr)[None, :]
    drow = np.clip(a - case + NA_ROWS - 1, 0, 2 * NA_ROWS - 2)
    bias = rpb[:, drow][:, :, :, dcol]
    bias = jnp.where(jnp.asarray(ok)[None, None, None], bias.astype(F32) * LOG2E, NEG_BIG)
    bias = bias.transpose(0, 1, 3, 2, 4)
    return bias.reshape(rpb.shape[0], kr, GRID_W, kr * GRID_W)


def _na_kernel(q_ref, k_ref, v_ref, bias_ref, y_ref, qs_ref, s_ref, p_ref, l_ref, o_ref, *, n_ctx, rows):
    w = GRID_W
    kr = min(NA_ROWS, rows)
    n_loc = kr * w
    grp = NA_ROW_GROUP
    lanes = 2 * NA_HEAD_DIM
    first = lax.broadcasted_iota(jnp.int32, (1, lanes), 1) < NA_HEAD_DIM

    def split_heads(qv):
        zero = jnp.zeros_like(qv)
        return jnp.concatenate([jnp.where(first, qv, zero), jnp.where(first, zero, qv)], axis=0)

    def merge_heads(o, n):
        return jnp.where(first, o[:n], o[n:])

    def window(r):
        rs = jnp.clip(r - kr // 2, 0, rows - kr)
        return r - rs, pl.ds(pl.multiple_of(n_ctx + rs * w, w), n_loc)

    kc = k_ref[0, 0:n_ctx, :]
    vc = v_ref[0, 0:n_ctx, :]

    s = _dot_nt(split_heads(q_ref[0, 0:n_ctx, :]), kc)
    p = jnp.exp2(s - jnp.max(s, axis=-1, keepdims=True))
    o = _dot(p.astype(BF16), vc) / jnp.sum(p, axis=-1, keepdims=True)
    y_ref[0, 0:n_ctx, :] = merge_heads(o, n_ctx).astype(y_ref.dtype)

    def prep(r, carry):
        qs_ref[r] = split_heads(q_ref[0, pl.ds(pl.multiple_of(n_ctx + r * w, w), w), :])
        return carry

    def local_scores(r, carry):
        case, keys = window(r)
        bias = jnp.concatenate([bias_ref[0, case], bias_ref[1, case]], axis=0)
        s_ref[r, :, 0:n_loc] = _dot_nt(qs_ref[r], k_ref[0, keys, :]) + bias
        return carry

    def ctx_scores(g, carry):
        blk = pl.ds(g * grp, grp)
        qg = qs_ref[blk].reshape(grp * 2 * w, lanes)
        s_ref[blk, :, n_loc:] = _dot_nt(qg, kc).reshape(grp, 2 * w, n_ctx)
        return carry

    def softmax(r, carry):
        sr = s_ref[r]
        pr = jnp.exp2(sr - jnp.max(sr, axis=-1, keepdims=True))
        l_ref[r] = jnp.broadcast_to(jnp.sum(pr, axis=-1, keepdims=True), (2 * w, lanes))
        p_ref[r] = pr.astype(BF16)
        return carry

    def ctx_values(g, carry):
        blk = pl.ds(g * grp, grp)
        pg = p_ref[blk, :, n_loc:].reshape(grp * 2 * w, n_ctx)
        o_ref[blk] = _dot(pg, vc).reshape(grp, 2 * w, lanes)
        return carry

    def local_values(r, carry):
        _, keys = window(r)
        orow = (_dot(p_ref[r, :, 0:n_loc], v_ref[0, keys, :]) + o_ref[r]) / l_ref[r]
        y_ref[0, pl.ds(pl.multiple_of(n_ctx + r * w, w), w), :] = merge_heads(orow, w).astype(y_ref.dtype)
        return carry

    lax.fori_loop(0, rows, prep, 0, unroll=8)
    lax.fori_loop(0, rows, local_scores, 0, unroll=8)
    lax.fori_loop(0, rows // grp, ctx_scores, 0, unroll=4)
    lax.fori_loop(0, rows, softmax, 0, unroll=4)
    lax.fori_loop(0, rows // grp, ctx_values, 0, unroll=4)
    lax.fori_loop(0, rows, local_values, 0, unroll=8)


def _na_attention(qkv, bias, *, n_ctx):
    b, t, n = qkv.shape
    d = n // 3
    lanes = 2 * NA_HEAD_DIM
    pairs = d // lanes
    rows = (t - n_ctx) // GRID_W
    kr = bias.shape[1]
    n_keys = kr * GRID_W + n_ctx
    assert rows % NA_ROW_GROUP == 0
    part = lambda p: pl.BlockSpec((1, t, lanes), lambda h, i: (i, 0, p * pairs + h))
    kern = functools.partial(_na_kernel, n_ctx=n_ctx, rows=rows)
    return pl.pallas_call(
        kern,
        grid=(pairs, b),
        in_specs=[
            part(0), part(1), part(2),
            pl.BlockSpec((2, kr, GRID_W, kr * GRID_W), lambda h, i: (h, 0, 0, 0)),
        ],
        out_specs=pl.BlockSpec((1, t, lanes), lambda h, i: (i, 0, h)),
        out_shape=jax.ShapeDtypeStruct((b, t, d), BF16),
        scratch_shapes=[
            pltpu.VMEM((rows, 2 * GRID_W, lanes), BF16),
            pltpu.VMEM((rows, 2 * GRID_W, n_keys), F32),
            pltpu.VMEM((rows, 2 * GRID_W, n_keys), BF16),
            pltpu.VMEM((rows, 2 * GRID_W, lanes), F32),
            pltpu.VMEM((rows, 2 * GRID_W, lanes), F32),
        ],
        compiler_params=pltpu.CompilerParams(
            dimension_semantics=("arbitrary", "arbitrary"), vmem_limit_bytes=VMEM_LIMIT),
        name="na_attention",
    )(qkv, qkv, qkv, bias)


def kernel(x, c, ctx, c_ctx, mod_w, mod_b, ln_g, ln_b, ffn_w_in, ffn_w_out, pool_w, pool_scale,
           hgrn_w_in, hgrn_lb_logits, hgrn_norm_w, hgrn_w_out, na_w_qkv, na_rpb, na_w_out):
    b, seq, d = x.shape
    n_ctx = ctx.shape[1]
    depth = mod_w.shape[0]
    assert n_ctx % TOKEN_TILE == 0 and seq % TOKEN_TILE == 0 and b < MOD_ROWS
    assert seq % GRID_W == 0 and n_ctx % HGRN_CHUNK == 0 and seq % HGRN_CHUNK == 0

    cc = jnp.concatenate([c, c_ctx[None, :], jnp.zeros((MOD_ROWS - b - 1, d), c.dtype)], axis=0)
    mod_all = _modulation(cc, mod_w, mod_b)
    xx = jnp.concatenate([ctx, x], axis=1)

    for i in range(depth):
        last = i == depth - 1
        kind = i % N_MIXERS
        j = i // N_MIXERS
        mod6 = mod_all[i].reshape(MOD_ROWS, N_MOD, d)
        w_in = ffn_w_in[i].astype(BF16)
        w_out = ffn_w_out[i].astype(BF16)
        if kind == 0:
            y = _pool_mixer(xx, mod6, pool_w[j], pool_scale[j], n_ctx)
            w_mix = None
        elif kind == 1:
            proj = _modulated_proj(xx, mod6, hgrn_w_in[j].astype(BF16), F32, n_chunk=d)
            y = _hgrn_scan(proj, hgrn_lb_logits, hgrn_norm_w[j], layer=i, n_ctx=n_ctx)
            w_mix = hgrn_w_out[j].astype(BF16)
        else:
            qkv = _modulated_proj(xx, mod6, na_w_qkv[j].astype(BF16), BF16, n_chunk=d,
                                  first_chunk_scale=NA_HEAD_DIM ** -0.5 * LOG2E)
            bias = _na_bias_table(na_rpb[j], seq // GRID_W)
            y = _na_attention(qkv, bias, n_ctx=n_ctx)
            w_mix = na_w_out[j].astype(BF16)
        xx = _ffn_block(xx, y, mod6, ln_g[i], ln_b[i], w_mix, w_in, w_out,
                        latent_only=last, n_ctx=n_ctx)
    return xx
```

```python
import functools

import numpy as np
import jax
import jax.numpy as jnp
from jax import lax
from jax.experimental import pallas as pl
from jax.experimental.pallas import tpu as pltpu

DEPTH = 4
N_MIXERS = 3
N_MOD = 6
ALPHA = (2 * DEPTH) ** 0.25
LN_EPS = 1e-6
RMS_EPS = 1e-6
POOL_WINDOWS = (2, 4, 8, 16)
HGRN_HEAD_DIM = 128
HGRN_CHUNK = 64
NA_HEAD_DIM = 64
NA_ROWS = 8
NA_COLS = 16
GRID_W = 64
NEG_BIG = -1e30
LOG2E = 1.4426950408889634
NA_ROW_GROUP = 4

TOKEN_TILE = 256
MOD_ROWS = 16
VMEM_LIMIT = 56 * 1024 * 1024

F32 = jnp.float32
BF16 = jnp.bfloat16


def _sigmoid(x):
    return 1.0 / (1.0 + jnp.exp(-x))


def _silu(x):
    return x * _sigmoid(x)


def _layer_norm(v, g, b):
    mu = jnp.mean(v, axis=-1, keepdims=True)
    d = v - mu
    var = jnp.mean(d * d, axis=-1, keepdims=True)
    return d * lax.rsqrt(var + LN_EPS) * g + b


def _dot(a, b):
    return jnp.dot(a, b, preferred_element_type=F32)


def _dot_nt(a, b):
    return lax.dot_general(a, b, (((1,), (1,)), ((), ())), preferred_element_type=F32)


def _dot_tn(a, b):
    return lax.dot_general(a, b, (((0,), (0,)), ((), ())), preferred_element_type=F32)


def _resident(shape, layer=None):
    zeros = (0,) * len(shape)
    if layer is None:
        return pl.BlockSpec(shape, lambda *_: zeros, pipeline_mode=pl.Buffered(1))
    return pl.BlockSpec((None,) + tuple(shape), lambda *_: (layer,) + zeros, pipeline_mode=pl.Buffered(1))


def _mod_kernel(c_ref, w_ref, b_ref, o_ref):
    s = _silu(c_ref[...]).astype(BF16)
    o_ref[0] = _dot(s, w_ref[0].astype(BF16)) + b_ref[0]


def _modulation(cc, mod_w, mod_b):
    depth, d, n = mod_w.shape
    tn = n // 4
    return pl.pallas_call(
        _mod_kernel,
        grid=(depth, n // tn),
        in_specs=[
            pl.BlockSpec((MOD_ROWS, d), lambda i, j: (0, 0)),
            pl.BlockSpec((1, d, tn), lambda i, j: (i, 0, j)),
            pl.BlockSpec((1, 1, tn), lambda i, j: (i, 0, j)),
        ],
        out_specs=pl.BlockSpec((1, MOD_ROWS, tn), lambda i, j: (i, 0, j)),
        out_shape=jax.ShapeDtypeStruct((depth, MOD_ROWS, n), F32),
        compiler_params=pltpu.CompilerParams(
            dimension_semantics=("arbitrary", "arbitrary"), vmem_limit_bytes=VMEM_LIMIT),
        name="modulation",
    )(cc, mod_w, mod_b.reshape(depth, 1, n))


def _pool_kernel(*refs, with_ctx, halo):
    if with_ctx:
        c_ref, x_ref, modc_ref, modx_ref, w_ref, s_ref, o_ref, pad_ref = refs
    else:
        x_ref, modx_ref, w_ref, s_ref, o_ref, pad_ref = refs
    g = pl.program_id(1)
    wg = w_ref[0].astype(BF16)
    scale = s_ref[...]

    def segment(src_ref, mod_ref, r0, window):
        length = src_ref.shape[1]
        lo = window // 2
        hi = window - 1 - lo
        h = src_ref[0] * (1.0 + mod_ref[0, 1:2, :]) + mod_ref[0, 0:1, :]
        zeros = jnp.zeros((halo, h.shape[1]), F32)
        pad_ref[0:halo, :] = zeros
        pad_ref[halo:halo + length, :] = h
        pad_ref[halo + length:2 * halo + length, :] = zeros
        tot = h
        for off in range(-lo, hi + 1):
            if off != 0:
                tot = tot + pad_ref[halo + off:halo + off + length, :]
        t = lax.broadcasted_iota(jnp.int32, h.shape, 0)
        cnt = jnp.minimum(t + hi + 1, length) - jnp.maximum(t - lo, 0)
        p = tot / cnt.astype(F32) - h
        o_ref[0, r0:r0 + length, :] = _dot(p.astype(BF16), wg) * scale

    for gi, window in enumerate(POOL_WINDOWS):
        @pl.when(g == gi)
        def _(window=window):
            if with_ctx:
                segment(c_ref, modc_ref, 0, window)
                segment(x_ref, modx_ref, c_ref.shape[1], window)
            else:
                segment(x_ref, modx_ref, 0, window)


def _pool_mixer(ctx, x, mod_all, layer, w_groups, scales, j):
    b, seq, d = x.shape
    _, ng, gd, _ = w_groups.shape
    halo = 8
    with_ctx = ctx is not None
    n_ctx = ctx.shape[1] if with_ctx else 0
    mod_spec = lambda row: pl.BlockSpec((None, 1, N_MOD, gd), lambda i, g: (layer, row(i), 0, g))
    in_specs, args = [], []
    if with_ctx:
        in_specs.append(pl.BlockSpec((1, n_ctx, gd), lambda i, g: (i, 0, g)))
        args.append(ctx)
    in_specs.append(pl.BlockSpec((1, seq, gd), lambda i, g: (i, 0, g)))
    args.append(x)
    if with_ctx:
        in_specs.append(mod_spec(lambda i: b))
        args.append(mod_all)
    in_specs += [
        mod_spec(lambda i: i),
        pl.BlockSpec((None, 1, gd, gd), lambda i, g: (j, g, 0, 0)),
        pl.BlockSpec((None, 1, gd), lambda i, g: (j, 0, g)),
    ]
    args += [mod_all, w_groups, scales.reshape(scales.shape[0], 1, d)]
    kern = functools.partial(_pool_kernel, with_ctx=with_ctx, halo=halo)
    return pl.pallas_call(
        kern,
        grid=(b, ng),
        in_specs=in_specs,
        out_specs=pl.BlockSpec((1, n_ctx + seq, gd), lambda i, g: (i, 0, g)),
        out_shape=jax.ShapeDtypeStruct((b, n_ctx + seq, d), F32),
        scratch_shapes=[pltpu.VMEM((seq + 2 * halo, gd), F32)],
        compiler_params=pltpu.CompilerParams(
            dimension_semantics=("arbitrary", "arbitrary"), vmem_limit_bytes=VMEM_LIMIT),
        name="pool_mixer",
    )(*args)


def _proj_kernel(c_ref, x_ref, mod_ref, w_ref, o_ref, *, n_ctx_tiles, n_chunk, first_chunk_scale):
    def run(src_ref):
        h = (src_ref[0] * (1.0 + mod_ref[0, 1:2, :]) + mod_ref[0, 0:1, :]).astype(BF16)
        n = w_ref.shape[1]
        for c0 in range(0, n, n_chunk):
            r = _dot(h, w_ref[:, c0:c0 + n_chunk])
            if c0 == 0 and first_chunk_scale != 1.0:
                r = r * first_chunk_scale
            o_ref[0, :, c0:c0 + n_chunk] = r.astype(o_ref.dtype)

    _per_segment(pl.program_id(1), n_ctx_tiles, lambda: run(c_ref), lambda: run(x_ref))


def _per_segment(j, n_ctx_tiles, ctx_fn, latent_fn):
    pl.when(j < n_ctx_tiles)(ctx_fn)
    pl.when(j >= n_ctx_tiles)(latent_fn)


def _segment_specs(b, d, n_ctx_tiles, layer):
    return [
        pl.BlockSpec((1, TOKEN_TILE, d), lambda i, j: (i, jnp.minimum(j, n_ctx_tiles - 1), 0)),
        pl.BlockSpec((1, TOKEN_TILE, d), lambda i, j: (i, jnp.maximum(j - n_ctx_tiles, 0), 0)),
        pl.BlockSpec((None, 1, N_MOD, d), lambda i, j: (layer, jnp.where(j < n_ctx_tiles, b, i), 0, 0)),
    ]


def _modulated_proj(ctx, x, mod_all, layer, w_all, j_w, out_dtype, n_chunk, first_chunk_scale=1.0):
    b, seq, d = x.shape
    n_ctx = ctx.shape[1]
    n = w_all.shape[2]
    nct = n_ctx // TOKEN_TILE
    kern = functools.partial(_proj_kernel, n_ctx_tiles=nct, n_chunk=n_chunk,
                             first_chunk_scale=first_chunk_scale)
    return pl.pallas_call(
        kern,
        grid=(b, (n_ctx + seq) // TOKEN_TILE),
        in_specs=_segment_specs(b, d, nct, layer) + [_resident((d, n), j_w)],
        out_specs=pl.BlockSpec((1, TOKEN_TILE, n), lambda i, j: (i, j, 0)),
        out_shape=jax.ShapeDtypeStruct((b, n_ctx + seq, n), out_dtype),
        compiler_params=pltpu.CompilerParams(
            dimension_semantics=("arbitrary", "arbitrary"), vmem_limit_bytes=VMEM_LIMIT),
        name="modulated_proj",
    )(ctx, x, mod_all, w_all)


def _ffn_kernel(*refs, with_ctx, n_ctx_tiles, has_mix, ffn_hidden):
    refs = list(refs)
    c_ref = refs.pop(0) if with_ctx else None
    x_ref, y_ref, mod_ref, lng_ref, lnb_ref = refs[:5]
    wmix_ref = refs[5] if has_mix else None
    win_ref, wout_ref = refs[5 + has_mix:7 + has_mix]
    outs = refs[7 + has_mix:]

    def run(src_ref, dst_ref):
        x = src_ref[0]
        g1 = mod_ref[0, 2:3, :]
        sh2 = mod_ref[0, 3:4, :]
        sc2 = mod_ref[0, 4:5, :]
        g2 = mod_ref[0, 5:6, :]
        if has_mix:
            ox = _dot(y_ref[0], wmix_ref[...])
        else:
            ox = y_ref[0]
        x1 = _layer_norm(ALPHA * x + g1 * ox, lng_ref[0:1, :], lnb_ref[0:1, :])
        h2 = (x1 * (1.0 + sc2) + sh2).astype(BF16)
        gu = _dot(h2, win_ref[...])
        act = (_silu(gu[:, :ffn_hidden]) * gu[:, ffn_hidden:]).astype(BF16)
        y2 = _dot(act, wout_ref[...])
        dst_ref[0] = _layer_norm(ALPHA * x1 + g2 * y2, lng_ref[1:2, :], lnb_ref[1:2, :])

    if with_ctx:
        _per_segment(pl.program_id(1), n_ctx_tiles,
                     lambda: run(c_ref, outs[0]), lambda: run(x_ref, outs[1]))
    else:
        run(x_ref, outs[0])


def _ffn_block(ctx, x, y, y_tile_offset, mod_all, layer, ln_g, ln_b, w_mix_all, j_mix, w_in_all, w_out_all):
    b, seq, d = x.shape
    f = w_out_all.shape[1]
    with_ctx = ctx is not None
    nct = ctx.shape[1] // TOKEN_TILE if with_ctx else 0
    n_tiles = nct + seq // TOKEN_TILE
    has_mix = w_mix_all is not None
    if with_ctx:
        in_specs = _segment_specs(b, d, nct, layer)
        args = [ctx, x, mod_all]
    else:
        in_specs = [pl.BlockSpec((1, TOKEN_TILE, d), lambda i, j: (i, j, 0)),
                    pl.BlockSpec((None, 1, N_MOD, d), lambda i, j: (layer, i, 0, 0))]
        args = [x, mod_all]
    in_specs.insert(len(in_specs) - 1, pl.BlockSpec((1, TOKEN_TILE, d), lambda i, j: (i, j + y_tile_offset, 0)))
    args.insert(len(args) - 1, y)
    in_specs += [_resident((2, d), layer), _resident((2, d), layer)]
    args += [ln_g, ln_b]
    if has_mix:
        in_specs.append(_resident((d, d), j_mix))
        args.append(w_mix_all)
    in_specs += [_resident((d, 2 * f), layer), _resident((f, d), layer)]
    args += [w_in_all, w_out_all]
    x_spec = pl.BlockSpec((1, TOKEN_TILE, d), lambda i, j: (i, jnp.maximum(j - nct, 0), 0))
    x_shape = jax.ShapeDtypeStruct((b, seq, d), F32)
    if with_ctx:
        out_specs = [pl.BlockSpec((1, TOKEN_TILE, d), lambda i, j: (i, jnp.minimum(j, nct - 1), 0)), x_spec]
        out_shape = [jax.ShapeDtypeStruct(ctx.shape, F32), x_shape]
    else:
        out_specs, out_shape = [x_spec], [x_shape]
    kern = functools.partial(_ffn_kernel, with_ctx=with_ctx, n_ctx_tiles=nct, has_mix=has_mix, ffn_hidden=f)
    outs = pl.pallas_call(
        kern,
        grid=(b, n_tiles),
        in_specs=in_specs,
        out_specs=out_specs,
        out_shape=out_shape,
        compiler_params=pltpu.CompilerParams(
            dimension_semantics=("arbitrary", "arbitrary"), vmem_limit_bytes=VMEM_LIMIT),
        name="ffn_block",
    )(*args)
    return (outs[0], outs[1]) if with_ctx else (None, outs[0])


def _hgrn_constants():
    c = HGRN_CHUNK
    t = np.arange(c)[:, None]
    u = np.arange(c)[None, :]
    mats = [u <= t]
    masks = [np.broadcast_to(t == u, (c, c))]
    m = c // 2
    while m >= 1:
        ref = (t // (2 * m)) * 2 * m + m - 1
        second = (t % (2 * m)) >= m
        mats.append(np.where(second, (u > ref) & (u <= t), (u > t) & (u <= ref)))
        masks.append(((t // (2 * m)) == (u // (2 * m))) & second & ((u % (2 * m)) < m))
        m //= 2
    mats.append(u > t)
    masks.append(np.zeros((c, c), bool))
    pair = lambda ms: np.stack([np.concatenate([ms[2 * p], ms[2 * p + 1]], 1) for p in range(len(ms) // 2)])
    flip = lambda ms: [a[::-1, ::-1] for a in ms]
    dmat = np.stack([np.concatenate(mats, 0), np.concatenate(flip(mats), 0)])
    dmat = np.concatenate([dmat] * 3, axis=2)
    mask = np.stack([pair(masks), pair(flip(masks))])
    return dmat.astype(np.float32), mask.astype(np.float32)


def _hgrn_scan_kernel(q_ref, v_ref, zf_ref, zb_ref, g_ref, lbl_ref, nw_ref, dm_ref, mk_ref,
                      y_ref, o_ref, qx_ref, ke_ref, r_ref, qf_ref, kf_ref, x_ref, u_ref, dec_ref, st_ref,
                      *, layer, n_ctx):
    c = HGRN_CHUNK
    hd = HGRN_HEAD_DIM
    t_all = q_ref.shape[1]
    n_chunks = t_all // c
    n_ctx_chunks = n_ctx // c
    n_pairs = mk_ref.shape[1]
    n_blocks = dm_ref.shape[1] // c

    def lower_bound(direction):
        rows = [lbl_ref[direction, j:j + 1, :] for j in range(lbl_ref.shape[1])]
        mx = functools.reduce(jnp.maximum, rows)
        e = [jnp.exp(r - mx) for r in rows]
        tot = functools.reduce(lambda a, b_: a + b_, e)
        p = [ei / tot for ei in e]
        acc = p[0]
        for j in range(1, layer + 1):
            acc = acc + p[j]
        return acc - p[0]

    lb = (lower_bound(0), lower_bound(1))
    z_refs = (zf_ref, zb_ref)
    end_row = (c - 1, 0)
    zeros_k = jnp.zeros((c, hd), BF16)

    def decays(i, carry):
        rows2 = pl.ds(pl.multiple_of(i * 2 * c, 2 * c), 2 * c)
        q = _silu(q_ref[0, rows2, :])
        qf_ref[rows2, :] = q
        for d in range(2):
            f = lb[d] + (1.0 - lb[d]) * _sigmoid(z_refs[d][0, rows2, :])
            k = 1.0 - f
            kf_ref[d, rows2, :] = k
            g = jnp.log(f) * LOG2E
            hi = g.astype(BF16)
            r1 = g - hi.astype(F32)
            mid = r1.astype(BF16)
            lo = (r1 - mid.astype(F32)).astype(BF16)
            side = lambda a: jnp.concatenate([a[:c], a[c:]], axis=1)
            x2 = jnp.exp2(_dot(dm_ref[d], jnp.concatenate([side(hi), side(mid), side(lo)], axis=0)))
            for cc in range(2):
                xs = x2[:, cc * hd:(cc + 1) * hd]
                rows = pl.ds(pl.multiple_of(i * 2 * c + cc * c, c), c)
                x_ref[d, 2 * i + cc] = xs[c:(n_blocks - 1) * c]
                qx_ref[d, rows, :] = (q[cc * c:(cc + 1) * c] * xs[0:c]).astype(BF16)
                ke_ref[d, rows, :] = (k[cc * c:(cc + 1) * c] * xs[(n_blocks - 1) * c:]).astype(BF16)
                dec_ref[d, 2 * i + cc] = jnp.broadcast_to(xs[end_row[d]:end_row[d] + 1], (8, hd))
        return carry

    def interactions(ci, carry):
        rows = pl.ds(pl.multiple_of(ci * c, c), c)
        qc = qf_ref[rows, :]
        for d in range(2):
            kc = kf_ref[d, rows, :]
            xs = x_ref[d, ci]
            lvl_q = [qc] + [qc * xs[l * c:(l + 1) * c] for l in range(n_blocks - 2)]
            lvl_k = [kc] + [kc * xs[l * c:(l + 1) * c] for l in range(n_blocks - 2)]
            r = None
            for p in range(n_pairs):
                la, lb_ = 2 * p, 2 * p + 1
                qa = lvl_q[la].astype(BF16)
                ka = lvl_k[la].astype(BF16)
                if lb_ < len(lvl_q):
                    qb = lvl_q[lb_].astype(BF16)
                    kb = lvl_k[lb_].astype(BF16)
                else:
                    qb = kb = zeros_k
                lhs = jnp.concatenate([qa, qb], axis=1)
                rhs = jnp.concatenate([jnp.concatenate([ka, zeros_k], axis=1),
                                       jnp.concatenate([zeros_k, kb], axis=1)], axis=0)
                rp = mk_ref[d, p] * _dot_nt(lhs, rhs)
                r = rp if r is None else r + rp
            r_ref[d, rows, :] = r.astype(BF16)
        return carry

    def values(ci, carry):
        rows = pl.ds(pl.multiple_of(ci * c, c), c)
        vc = v_ref[0, rows, :].astype(BF16)
        oo = _dot(jnp.concatenate([r_ref[0, rows, :], r_ref[1, rows, :]], axis=0),
                  jnp.concatenate([vc, vc], axis=0))
        o_ref[0, rows, :] = oo[:c]
        o_ref[1, rows, :] = oo[c:]
        uu = _dot_tn(vc, jnp.concatenate([ke_ref[0, rows, :], ke_ref[1, rows, :]], axis=1))
        u_ref[0, ci] = uu[:, :hd]
        u_ref[1, ci] = uu[:, hd:]
        return carry

    lax.fori_loop(0, n_chunks // 2, decays, 0, unroll=3)
    lax.fori_loop(0, n_chunks, interactions, 0, unroll=4)
    lax.fori_loop(0, n_chunks, values, 0, unroll=6)

    st_ref[...] = jnp.zeros(st_ref.shape, F32)

    def inter(i, carry):
        cb = jnp.where(i < n_ctx_chunks, n_ctx_chunks - 1 - i, n_chunks - 1 - (i - n_ctx_chunks))
        for d, ci in ((0, i), (1, cb)):
            rows = pl.ds(pl.multiple_of(ci * c, c), c)
            st = st_ref[d]
            o_ref[d, rows, :] += _dot_nt(qx_ref[d, rows, :], st.astype(BF16))
            st_ref[d] = st * dec_ref[d, ci][0:1] + u_ref[d, ci]
        return carry

    lax.fori_loop(0, n_chunks, inter, 0, unroll=6)

    nw = nw_ref[...]

    def finish(j, carry):
        rows = pl.ds(pl.multiple_of(j * TOKEN_TILE, TOKEN_TILE), TOKEN_TILE)
        o = o_ref[0, rows, :] + o_ref[1, rows, :]
        o = o * lax.rsqrt(jnp.mean(o * o, axis=-1, keepdims=True) + RMS_EPS)
        gate = _silu(g_ref[0, rows, :])
        y_ref[0, rows, :] = (o * nw * gate).astype(y_ref.dtype)
        return carry

    lax.fori_loop(0, t_all // TOKEN_TILE, finish, 0)


def _hgrn_scan(proj, lb_logits, norm_w_all, j_w, *, layer, n_ctx):
    b, t, n = proj.shape
    d = n // 5
    hd = HGRN_HEAD_DIM
    heads = d // hd
    n_chunks = t // HGRN_CHUNK
    assert n_chunks % 4 == 0
    dmat, mask = _hgrn_constants()
    part = lambda p: pl.BlockSpec((1, t, hd), lambda i, h: (i, 0, p * heads + h))
    kern = functools.partial(_hgrn_scan_kernel, layer=layer, n_ctx=n_ctx)
    return pl.pallas_call(
        kern,
        grid=(b, heads),
        in_specs=[
            part(0), part(1), part(2), part(3), part(4),
            pl.BlockSpec((2, lb_logits.shape[1], hd), lambda i, h: (0, 0, h)),
            pl.BlockSpec((None, 1, hd), lambda i, h: (j_w, 0, h)),
            _resident(dmat.shape),
            _resident(mask.shape),
        ],
        out_specs=pl.BlockSpec((1, t, hd), lambda i, h: (i, 0, h)),
        out_shape=jax.ShapeDtypeStruct((b, t, d), BF16),
        scratch_shapes=[
            pltpu.VMEM((2, t, hd), F32),
            pltpu.VMEM((2, t, hd), BF16),
            pltpu.VMEM((2, t, hd), BF16),
            pltpu.VMEM((2, t, hd), BF16),
            pltpu.VMEM((t, hd), F32),
            pltpu.VMEM((2, t, hd), F32),
            pltpu.VMEM((2, n_chunks, (dmat.shape[1] // HGRN_CHUNK - 2) * HGRN_CHUNK, hd), F32),
            pltpu.VMEM((2, n_chunks, hd, hd), F32),
            pltpu.VMEM((2, n_chunks, 8, hd), F32),
            pltpu.VMEM((2, hd, hd), F32),
        ],
        compiler_params=pltpu.CompilerParams(
            dimension_semantics=("arbitrary", "arbitrary"), vmem_limit_bytes=VMEM_LIMIT),
        name="hgrn_scan",
    )(proj, proj, proj, proj, proj, lb_logits, norm_w_all.reshape(norm_w_all.shape[0], 1, d),
      jnp.asarray(dmat, BF16), jnp.asarray(mask, F32))


def _na_bias_table(rpb):
    qcol = np.arange(GRID_W)[:, None]
    kcol = np.arange(GRID_W)[None, :]
    win = np.clip(qcol - NA_COLS // 2, 0, GRID_W - NA_COLS)
    ok = (kcol >= win) & (kcol < win + NA_COLS)
    dcol = np.arange(2 * NA_COLS - 1)[:, None, None]
    select = ((kcol - qcol + NA_COLS - 1)[None] == dcol) & ok[None]
    per_row = jnp.einsum('hrc,cqk->hrqk', rpb.astype(F32), jnp.asarray(select, F32),
                         precision=lax.Precision.HIGHEST)
    per_row = jnp.where(jnp.asarray(ok)[None, None], per_row * LOG2E, NEG_BIG)
    return jnp.concatenate([per_row[:, :-1], per_row[:, 1:]], axis=-1)


def _na_kernel(q_ref, k_ref, v_ref, bias_ref, y_ref, qs_ref, s_ref, p_ref, l_ref, o_ref, *, n_ctx, rows):
    w = GRID_W
    kr = min(NA_ROWS, rows)
    n_loc = kr * w
    grp = NA_ROW_GROUP
    lanes = 2 * NA_HEAD_DIM
    first = lax.broadcasted_iota(jnp.int32, (1, lanes), 1) < NA_HEAD_DIM

    def split_heads(qv):
        zero = jnp.zeros_like(qv)
        return jnp.concatenate([jnp.where(first, qv, zero), jnp.where(first, zero, qv)], axis=0)

    def merge_heads(o, n):
        return jnp.where(first, o[:n], o[n:])

    def window(r):
        rs = jnp.clip(r - kr // 2, 0, rows - kr)
        return r - rs, pl.ds(pl.multiple_of(n_ctx + rs * w, w), n_loc)

    kc = k_ref[0, 0:n_ctx, :]
    vc = v_ref[0, 0:n_ctx, :]

    s = _dot_nt(split_heads(q_ref[0, 0:n_ctx, :]), kc)
    p = jnp.exp2(s - jnp.max(s, axis=-1, keepdims=True))
    o = _dot(p.astype(BF16), vc) / jnp.sum(p, axis=-1, keepdims=True)
    y_ref[0, 0:n_ctx, :] = merge_heads(o, n_ctx).astype(y_ref.dtype)

    def prep(r, carry):
        qs_ref[r] = split_heads(q_ref[0, pl.ds(pl.multiple_of(n_ctx + r * w, w), w), :])
        return carry

    def local_scores(r, carry):
        case, keys = window(r)
        head_bias = lambda hh: jnp.concatenate(
            [bias_ref[hh, NA_ROWS - 1 - case + a] for a in range(0, kr, 2)], axis=1)
        bias = jnp.concatenate([head_bias(0), head_bias(1)], axis=0)
        s_ref[r, :, 0:n_loc] = _dot_nt(qs_ref[r], k_ref[0, keys, :]) + bias
        return carry

    def ctx_scores(g, carry):
        blk = pl.ds(g * grp, grp)
        qg = qs_ref[blk].reshape(grp * 2 * w, lanes)
        s_ref[blk, :, n_loc:] = _dot_nt(qg, kc).reshape(grp, 2 * w, n_ctx)
        return carry

    def softmax(r, carry):
        sr = s_ref[r]
        pr = jnp.exp2(sr - jnp.max(sr, axis=-1, keepdims=True))
        l_ref[r] = jnp.broadcast_to(jnp.sum(pr, axis=-1, keepdims=True), (2 * w, lanes))
        p_ref[r] = pr.astype(BF16)
        return carry

    def ctx_values(g, carry):
        blk = pl.ds(g * grp, grp)
        pg = p_ref[blk, :, n_loc:].reshape(grp * 2 * w, n_ctx)
        o_ref[blk] = _dot(pg, vc).reshape(grp, 2 * w, lanes)
        return carry

    def local_values(r, carry):
        _, keys = window(r)
        orow = (_dot(p_ref[r, :, 0:n_loc], v_ref[0, keys, :]) + o_ref[r]) / l_ref[r]
        y_ref[0, pl.ds(pl.multiple_of(n_ctx + r * w, w), w), :] = merge_heads(orow, w).astype(y_ref.dtype)
        return carry

    lax.fori_loop(0, rows, prep, 0, unroll=8)
    lax.fori_loop(0, rows, local_scores, 0, unroll=8)
    lax.fori_loop(0, rows // grp, ctx_scores, 0, unroll=4)
    lax.fori_loop(0, rows, softmax, 0, unroll=4)
    lax.fori_loop(0, rows // grp, ctx_values, 0, unroll=4)
    lax.fori_loop(0, rows, local_values, 0, unroll=8)


def _na_attention(qkv, bias, *, n_ctx):
    b, t, n = qkv.shape
    d = n // 3
    lanes = 2 * NA_HEAD_DIM
    pairs = d // lanes
    rows = (t - n_ctx) // GRID_W
    assert rows % NA_ROW_GROUP == 0 and rows >= NA_ROWS and NA_ROWS % 2 == 0
    n_keys = NA_ROWS * GRID_W + n_ctx
    part = lambda p: pl.BlockSpec((1, t, lanes), lambda h, i: (i, 0, p * pairs + h))
    kern = functools.partial(_na_kernel, n_ctx=n_ctx, rows=rows)
    return pl.pallas_call(
        kern,
        grid=(pairs, b),
        in_specs=[
            part(0), part(1), part(2),
            pl.BlockSpec((2,) + bias.shape[1:], lambda h, i: (h, 0, 0, 0)),
        ],
        out_specs=pl.BlockSpec((1, t, lanes), lambda h, i: (i, 0, h)),
        out_shape=jax.ShapeDtypeStruct((b, t, d), BF16),
        scratch_shapes=[
            pltpu.VMEM((rows, 2 * GRID_W, lanes), BF16),
            pltpu.VMEM((rows, 2 * GRID_W, n_keys), F32),
            pltpu.VMEM((rows, 2 * GRID_W, n_keys), BF16),
            pltpu.VMEM((rows, 2 * GRID_W, lanes), F32),
            pltpu.VMEM((rows, 2 * GRID_W, lanes), F32),
        ],
        compiler_params=pltpu.CompilerParams(
            dimension_semantics=("arbitrary", "arbitrary"), vmem_limit_bytes=VMEM_LIMIT),
        name="na_attention",
    )(qkv, qkv, qkv, bias)


def kernel(x, c, ctx, c_ctx, mod_w, mod_b, ln_g, ln_b, ffn_w_in, ffn_w_out, pool_w, pool_scale,
           hgrn_w_in, hgrn_lb_logits, hgrn_norm_w, hgrn_w_out, na_w_qkv, na_rpb, na_w_out):
    b, seq, d = x.shape
    n_ctx = ctx.shape[1]
    depth = mod_w.shape[0]
    assert n_ctx % TOKEN_TILE == 0 and seq % TOKEN_TILE == 0 and b < MOD_ROWS
    assert seq % GRID_W == 0 and n_ctx % HGRN_CHUNK == 0 and seq % HGRN_CHUNK == 0

    cc = jnp.concatenate([c, c_ctx[None, :], jnp.zeros((MOD_ROWS - b - 1, d), c.dtype)], axis=0)
    mod_all = _modulation(cc, mod_w, mod_b).reshape(depth, MOD_ROWS, N_MOD, d)
    w_in_all = ffn_w_in.astype(BF16)
    w_out_all = ffn_w_out.astype(BF16)
    hgrn_w_in_all = hgrn_w_in.astype(BF16)
    hgrn_w_out_all = hgrn_w_out.astype(BF16)
    na_w_qkv_all = na_w_qkv.astype(BF16)
    na_w_out_all = na_w_out.astype(BF16)
    n_ctx_tiles = n_ctx // TOKEN_TILE

    for i in range(depth):
        last = i == depth - 1
        kind = i % N_MIXERS
        j = i // N_MIXERS
        w_mix_all = None
        y_tile_offset = n_ctx_tiles if last else 0
        if kind == 0:
            y = _pool_mixer(None if last else ctx, x, mod_all, i, pool_w, pool_scale, j)
            y_tile_offset = 0
        elif kind == 1:
            proj = _modulated_proj(ctx, x, mod_all, i, hgrn_w_in_all, j, F32, n_chunk=d)
            y = _hgrn_scan(proj, hgrn_lb_logits, hgrn_norm_w, j, layer=i, n_ctx=n_ctx)
            w_mix_all = hgrn_w_out_all
        else:
            qkv = _modulated_proj(ctx, x, mod_all, i, na_w_qkv_all, j, BF16, n_chunk=d,
                                  first_chunk_scale=NA_HEAD_DIM ** -0.5 * LOG2E)
            y = _na_attention(qkv, _na_bias_table(na_rpb[j]), n_ctx=n_ctx)
            w_mix_all = na_w_out_all
        ctx, x = _ffn_block(None if last else ctx, x, y, y_tile_offset, mod_all, i, ln_g, ln_b,
                            w_mix_all, j, w_in_all, w_out_all)
    return x
```

```python
import functools

import numpy as np
import jax
import jax.numpy as jnp
from jax import lax
from jax.experimental import pallas as pl
from jax.experimental.pallas import tpu as pltpu

DEPTH = 4
N_MIXERS = 3
N_MOD = 6
ALPHA = (2 * DEPTH) ** 0.25
LN_EPS = 1e-6
RMS_EPS = 1e-6
POOL_WINDOWS = (2, 4, 8, 16)
HGRN_HEAD_DIM = 128
HGRN_CHUNK = 64
NA_HEAD_DIM = 64
NA_ROWS = 8
NA_COLS = 16
GRID_W = 64
NEG_BIG = -1e30
LOG2E = 1.4426950408889634
NA_ROW_GROUP = 4

TOKEN_TILE = 256
MOD_ROWS = 16
VMEM_LIMIT = 56 * 1024 * 1024

F32 = jnp.float32
BF16 = jnp.bfloat16


def _sigmoid(x):
    return 1.0 / (1.0 + jnp.exp(-x))


def _silu(x):
    return x * _sigmoid(x)


def _layer_norm(v, g, b):
    mu = jnp.mean(v, axis=-1, keepdims=True)
    d = v - mu
    var = jnp.mean(d * d, axis=-1, keepdims=True)
    return d * lax.rsqrt(var + LN_EPS) * g + b


def _dot(a, b):
    return jnp.dot(a, b, preferred_element_type=F32)


def _dot_nt(a, b):
    return lax.dot_general(a, b, (((1,), (1,)), ((), ())), preferred_element_type=F32)


def _dot_tn(a, b):
    return lax.dot_general(a, b, (((0,), (0,)), ((), ())), preferred_element_type=F32)


def _resident(shape, layer=None):
    zeros = (0,) * len(shape)
    if layer is None:
        return pl.BlockSpec(shape, lambda *_: zeros, pipeline_mode=pl.Buffered(1))
    return pl.BlockSpec((None,) + tuple(shape), lambda *_: (layer,) + zeros, pipeline_mode=pl.Buffered(1))


def _mod_kernel(c_ref, w_ref, b_ref, o_ref):
    s = _silu(c_ref[...]).astype(BF16)
    o_ref[0] = _dot(s, w_ref[0].astype(BF16)) + b_ref[0]


def _modulation(cc, mod_w, mod_b):
    depth, d, n = mod_w.shape
    tn = n // 4
    return pl.pallas_call(
        _mod_kernel,
        grid=(depth, n // tn),
        in_specs=[
            pl.BlockSpec((MOD_ROWS, d), lambda i, j: (0, 0)),
            pl.BlockSpec((1, d, tn), lambda i, j: (i, 0, j)),
            pl.BlockSpec((1, 1, tn), lambda i, j: (i, 0, j)),
        ],
        out_specs=pl.BlockSpec((1, MOD_ROWS, tn), lambda i, j: (i, 0, j)),
        out_shape=jax.ShapeDtypeStruct((depth, MOD_ROWS, n), F32),
        compiler_params=pltpu.CompilerParams(
            dimension_semantics=("arbitrary", "arbitrary"), vmem_limit_bytes=VMEM_LIMIT),
        name="modulation",
    )(cc, mod_w, mod_b.reshape(depth, 1, n))


def _pool_kernel(*refs, with_ctx, halo):
    if with_ctx:
        c_ref, x_ref, modc_ref, modx_ref, w_ref, s_ref, o_ref, pad_ref = refs
    else:
        x_ref, modx_ref, w_ref, s_ref, o_ref, pad_ref = refs
    g = pl.program_id(1)
    wg = w_ref[0].astype(BF16)
    scale = s_ref[...]

    def segment(src_ref, mod_ref, r0, window):
        length = src_ref.shape[1]
        lo = window // 2
        hi = window - 1 - lo
        h = src_ref[0] * (1.0 + mod_ref[0, 1:2, :]) + mod_ref[0, 0:1, :]
        zeros = jnp.zeros((halo, h.shape[1]), F32)
        pad_ref[0:halo, :] = zeros
        pad_ref[halo:halo + length, :] = h
        pad_ref[halo + length:2 * halo + length, :] = zeros
        tot = h
        for off in range(-lo, hi + 1):
            if off != 0:
                tot = tot + pad_ref[halo + off:halo + off + length, :]
        t = lax.broadcasted_iota(jnp.int32, h.shape, 0)
        cnt = jnp.minimum(t + hi + 1, length) - jnp.maximum(t - lo, 0)
        p = tot / cnt.astype(F32) - h
        o_ref[0, r0:r0 + length, :] = _dot(p.astype(BF16), wg) * scale

    for gi, window in enumerate(POOL_WINDOWS):
        @pl.when(g == gi)
        def _(window=window):
            if with_ctx:
                segment(c_ref, modc_ref, 0, window)
                segment(x_ref, modx_ref, c_ref.shape[1], window)
            else:
                segment(x_ref, modx_ref, 0, window)


def _pool_mixer(ctx, x, mod_all, layer, w_groups, scales, j):
    b, seq, d = x.shape
    _, ng, gd, _ = w_groups.shape
    halo = 8
    with_ctx = ctx is not None
    n_ctx = ctx.shape[1] if with_ctx else 0
    mod_spec = lambda row: pl.BlockSpec((None, 1, N_MOD, gd), lambda i, g: (layer, row(i), 0, g))
    in_specs, args = [], []
    if with_ctx:
        in_specs.append(pl.BlockSpec((1, n_ctx, gd), lambda i, g: (i, 0, g)))
        args.append(ctx)
    in_specs.append(pl.BlockSpec((1, seq, gd), lambda i, g: (i, 0, g)))
    args.append(x)
    if with_ctx:
        in_specs.append(mod_spec(lambda i: b))
        args.append(mod_all)
    in_specs += [
        mod_spec(lambda i: i),
        pl.BlockSpec((None, 1, gd, gd), lambda i, g: (j, g, 0, 0)),
        pl.BlockSpec((None, 1, gd), lambda i, g: (j, 0, g)),
    ]
    args += [mod_all, w_groups, scales.reshape(scales.shape[0], 1, d)]
    kern = functools.partial(_pool_kernel, with_ctx=with_ctx, halo=halo)
    return pl.pallas_call(
        kern,
        grid=(b, ng),
        in_specs=in_specs,
        out_specs=pl.BlockSpec((1, n_ctx + seq, gd), lambda i, g: (i, 0, g)),
        out_shape=jax.ShapeDtypeStruct((b, n_ctx + seq, d), F32),
        scratch_shapes=[pltpu.VMEM((seq + 2 * halo, gd), F32)],
        compiler_params=pltpu.CompilerParams(
            dimension_semantics=("arbitrary", "arbitrary"), vmem_limit_bytes=VMEM_LIMIT),
        name="pool_mixer",
    )(*args)


def _proj_kernel(c_ref, x_ref, mod_ref, w_ref, o_ref, *, n_ctx_tiles, n_chunk, first_chunk_scale):
    def run(src_ref):
        h = (src_ref[0] * (1.0 + mod_ref[0, 1:2, :]) + mod_ref[0, 0:1, :]).astype(BF16)
        n = w_ref.shape[1]
        for c0 in range(0, n, n_chunk):
            r = _dot(h, w_ref[:, c0:c0 + n_chunk])
            if c0 == 0 and first_chunk_scale != 1.0:
                r = r * first_chunk_scale
            o_ref[0, :, c0:c0 + n_chunk] = r.astype(o_ref.dtype)

    _per_segment(pl.program_id(1), n_ctx_tiles, lambda: run(c_ref), lambda: run(x_ref))


def _per_segment(j, n_ctx_tiles, ctx_fn, latent_fn):
    pl.when(j < n_ctx_tiles)(ctx_fn)
    pl.when(j >= n_ctx_tiles)(latent_fn)


def _segment_specs(b, d, n_ctx_tiles, layer):
    return [
        pl.BlockSpec((1, TOKEN_TILE, d), lambda i, j: (i, jnp.minimum(j, n_ctx_tiles - 1), 0)),
        pl.BlockSpec((1, TOKEN_TILE, d), lambda i, j: (i, jnp.maximum(j - n_ctx_tiles, 0), 0)),
        pl.BlockSpec((None, 1, N_MOD, d), lambda i, j: (layer, jnp.where(j < n_ctx_tiles, b, i), 0, 0)),
    ]


def _modulated_proj(ctx, x, mod_all, layer, w_all, j_w, out_dtype, n_chunk, first_chunk_scale=1.0):
    b, seq, d = x.shape
    n_ctx = ctx.shape[1]
    n = w_all.shape[2]
    nct = n_ctx // TOKEN_TILE
    kern = functools.partial(_proj_kernel, n_ctx_tiles=nct, n_chunk=n_chunk,
                             first_chunk_scale=first_chunk_scale)
    return pl.pallas_call(
        kern,
        grid=(b, (n_ctx + seq) // TOKEN_TILE),
        in_specs=_segment_specs(b, d, nct, layer) + [_resident((d, n), j_w)],
        out_specs=pl.BlockSpec((1, TOKEN_TILE, n), lambda i, j: (i, j, 0)),
        out_shape=jax.ShapeDtypeStruct((b, n_ctx + seq, n), out_dtype),
        compiler_params=pltpu.CompilerParams(
            dimension_semantics=("arbitrary", "arbitrary"), vmem_limit_bytes=VMEM_LIMIT),
        name="modulated_proj",
    )(ctx, x, mod_all, w_all)


def _ffn_kernel(*refs, with_ctx, n_ctx_tiles, has_mix, ffn_hidden):
    refs = list(refs)
    c_ref = refs.pop(0) if with_ctx else None
    x_ref, y_ref, mod_ref, lng_ref, lnb_ref = refs[:5]
    wmix_ref = refs[5] if has_mix else None
    win_ref, wout_ref = refs[5 + has_mix:7 + has_mix]
    outs = refs[7 + has_mix:]

    def run(src_ref, dst_ref):
        x = src_ref[0]
        g1 = mod_ref[0, 2:3, :]
        sh2 = mod_ref[0, 3:4, :]
        sc2 = mod_ref[0, 4:5, :]
        g2 = mod_ref[0, 5:6, :]
        if has_mix:
            ox = _dot(y_ref[0], wmix_ref[...])
        else:
            ox = y_ref[0]
        x1 = _layer_norm(ALPHA * x + g1 * ox, lng_ref[0:1, :], lnb_ref[0:1, :])
        h2 = (x1 * (1.0 + sc2) + sh2).astype(BF16)
        gu = _dot(h2, win_ref[...])
        act = (_silu(gu[:, :ffn_hidden]) * gu[:, ffn_hidden:]).astype(BF16)
        y2 = _dot(act, wout_ref[...])
        dst_ref[0] = _layer_norm(ALPHA * x1 + g2 * y2, lng_ref[1:2, :], lnb_ref[1:2, :])

    if with_ctx:
        _per_segment(pl.program_id(1), n_ctx_tiles,
                     lambda: run(c_ref, outs[0]), lambda: run(x_ref, outs[1]))
    else:
        run(x_ref, outs[0])


def _ffn_block(ctx, x, y, y_tile_offset, mod_all, layer, ln_g, ln_b, w_mix_all, j_mix, w_in_all, w_out_all):
    b, seq, d = x.shape
    f = w_out_all.shape[1]
    with_ctx = ctx is not None
    nct = ctx.shape[1] // TOKEN_TILE if with_ctx else 0
    n_tiles = nct + seq // TOKEN_TILE
    has_mix = w_mix_all is not None
    if with_ctx:
        in_specs = _segment_specs(b, d, nct, layer)
        args = [ctx, x, mod_all]
    else:
        in_specs = [pl.BlockSpec((1, TOKEN_TILE, d), lambda i, j: (i, j, 0)),
                    pl.BlockSpec((None, 1, N_MOD, d), lambda i, j: (layer, i, 0, 0))]
        args = [x, mod_all]
    in_specs.insert(len(in_specs) - 1, pl.BlockSpec((1, TOKEN_TILE, d), lambda i, j: (i, j + y_tile_offset, 0)))
    args.insert(len(args) - 1, y)
    in_specs += [_resident((2, d), layer), _resident((2, d), layer)]
    args += [ln_g, ln_b]
    if has_mix:
        in_specs.append(_resident((d, d), j_mix))
        args.append(w_mix_all)
    in_specs += [_resident((d, 2 * f), layer), _resident((f, d), layer)]
    args += [w_in_all, w_out_all]
    x_spec = pl.BlockSpec((1, TOKEN_TILE, d), lambda i, j: (i, jnp.maximum(j - nct, 0), 0))
    x_shape = jax.ShapeDtypeStruct((b, seq, d), F32)
    if with_ctx:
        out_specs = [pl.BlockSpec((1, TOKEN_TILE, d), lambda i, j: (i, jnp.minimum(j, nct - 1), 0)), x_spec]
        out_shape = [jax.ShapeDtypeStruct(ctx.shape, F32), x_shape]
    else:
        out_specs, out_shape = [x_spec], [x_shape]
    kern = functools.partial(_ffn_kernel, with_ctx=with_ctx, n_ctx_tiles=nct, has_mix=has_mix, ffn_hidden=f)
    outs = pl.pallas_call(
        kern,
        grid=(b, n_tiles),
        in_specs=in_specs,
        out_specs=out_specs,
        out_shape=out_shape,
        compiler_params=pltpu.CompilerParams(
            dimension_semantics=("arbitrary", "arbitrary"), vmem_limit_bytes=VMEM_LIMIT),
        name="ffn_block",
    )(*args)
    return (outs[0], outs[1]) if with_ctx else (None, outs[0])


def _hgrn_constants():
    c = HGRN_CHUNK
    t = np.arange(c)[:, None]
    u = np.arange(c)[None, :]
    mats = [u <= t]
    masks = [np.broadcast_to(t == u, (c, c))]
    m = c // 2
    while m >= 1:
        ref = (t // (2 * m)) * 2 * m + m - 1
        second = (t % (2 * m)) >= m
        mats.append(np.where(second, (u > ref) & (u <= t), (u > t) & (u <= ref)))
        masks.append(((t // (2 * m)) == (u // (2 * m))) & second & ((u % (2 * m)) < m))
        m //= 2
    mats.append(u > t)
    masks.append(np.zeros((c, c), bool))
    pair = lambda ms: np.stack([np.concatenate([ms[2 * p], ms[2 * p + 1]], 1) for p in range(len(ms) // 2)])
    flip = lambda ms: [a[::-1, ::-1] for a in ms]
    dmat = np.stack([np.concatenate(mats, 0), np.concatenate(flip(mats), 0)])
    dmat = np.concatenate([dmat] * 3, axis=2)
    mask = np.stack([pair(masks), pair(flip(masks))])
    return dmat.astype(np.float32), mask.astype(np.float32)


def _hgrn_scan_kernel(q_ref, v_ref, zf_ref, zb_ref, g_ref, lbl_ref, nw_ref, dm_ref, mk_ref,
                      y_ref, o_ref, qx_ref, ke_ref, r_ref, qf_ref, kf_ref, x_ref, u_ref, dec_ref, st_ref,
                      *, layer, n_ctx):
    c = HGRN_CHUNK
    hd = HGRN_HEAD_DIM
    t_all = q_ref.shape[1]
    n_chunks = t_all // c
    n_ctx_chunks = n_ctx // c
    n_pairs = mk_ref.shape[1]
    n_blocks = dm_ref.shape[1] // c

    def lower_bound(direction):
        rows = [lbl_ref[direction, j:j + 1, :] for j in range(lbl_ref.shape[1])]
        mx = functools.reduce(jnp.maximum, rows)
        e = [jnp.exp(r - mx) for r in rows]
        tot = functools.reduce(lambda a, b_: a + b_, e)
        p = [ei / tot for ei in e]
        acc = p[0]
        for j in range(1, layer + 1):
            acc = acc + p[j]
        return acc - p[0]

    lb = (lower_bound(0), lower_bound(1))
    z_refs = (zf_ref, zb_ref)
    end_row = (c - 1, 0)
    zeros_k = jnp.zeros((c, hd), BF16)

    def decays(i, slot):
        rows2 = pl.ds(pl.multiple_of(i * 2 * c, 2 * c), 2 * c)
        q = _silu(q_ref[0, rows2, :])
        qf_ref[slot] = q
        for d in range(2):
            f = lb[d] + (1.0 - lb[d]) * _sigmoid(z_refs[d][0, rows2, :])
            k = 1.0 - f
            kf_ref[slot, d] = k
            g = jnp.log(f) * LOG2E
            hi = g.astype(BF16)
            r1 = g - hi.astype(F32)
            mid = r1.astype(BF16)
            lo = (r1 - mid.astype(F32)).astype(BF16)
            side = lambda a: jnp.concatenate([a[:c], a[c:]], axis=1)
            x2 = jnp.exp2(_dot(dm_ref[d], jnp.concatenate([side(hi), side(mid), side(lo)], axis=0)))
            for cc in range(2):
                xs = x2[:, cc * hd:(cc + 1) * hd]
                rows = pl.ds(pl.multiple_of(i * 2 * c + cc * c, c), c)
                x_ref[slot, d, cc] = xs[c:(n_blocks - 1) * c]
                qx_ref[d, rows, :] = (q[cc * c:(cc + 1) * c] * xs[0:c]).astype(BF16)
                ke_ref[d, rows, :] = (k[cc * c:(cc + 1) * c] * xs[(n_blocks - 1) * c:]).astype(BF16)
                dec_ref[d, 2 * i + cc] = jnp.broadcast_to(xs[end_row[d]:end_row[d] + 1], (8, hd))

    def interactions(i, slot):
        for cc in range(2):
            rows = pl.ds(pl.multiple_of(i * 2 * c + cc * c, c), c)
            qc = qf_ref[slot, cc * c:(cc + 1) * c, :]
            for d in range(2):
                kc = kf_ref[slot, d, cc * c:(cc + 1) * c, :]
                xs = x_ref[slot, d, cc]
                lvl_q = [qc] + [qc * xs[l * c:(l + 1) * c] for l in range(n_blocks - 2)]
                lvl_k = [kc] + [kc * xs[l * c:(l + 1) * c] for l in range(n_blocks - 2)]
                r = None
                for p in range(n_pairs):
                    la, lb_ = 2 * p, 2 * p + 1
                    qa = lvl_q[la].astype(BF16)
                    ka = lvl_k[la].astype(BF16)
                    if lb_ < len(lvl_q):
                        qb = lvl_q[lb_].astype(BF16)
                        kb = lvl_k[lb_].astype(BF16)
                    else:
                        qb = kb = zeros_k
                    lhs = jnp.concatenate([qa, qb], axis=1)
                    rhs = jnp.concatenate([jnp.concatenate([ka, zeros_k], axis=1),
                                           jnp.concatenate([zeros_k, kb], axis=1)], axis=0)
                    rp = mk_ref[d, p] * _dot_nt(lhs, rhs)
                    r = rp if r is None else r + rp
                r_ref[d, rows, :] = r.astype(BF16)

    def values(ci, carry):
        rows = pl.ds(pl.multiple_of(ci * c, c), c)
        vc = v_ref[0, rows, :].astype(BF16)
        oo = _dot(jnp.concatenate([r_ref[0, rows, :], r_ref[1, rows, :]], axis=0),
                  jnp.concatenate([vc, vc], axis=0))
        o_ref[0, rows, :] = oo[:c]
        o_ref[1, rows, :] = oo[c:]
        uu = _dot_tn(vc, jnp.concatenate([ke_ref[0, rows, :], ke_ref[1, rows, :]], axis=1))
        u_ref[0, ci] = uu[:, :hd]
        u_ref[1, ci] = uu[:, hd:]
        return carry

    n_cpairs = n_chunks // 2

    def pipelined(k, carry):
        decays(2 * k + 1, 1)
        interactions(2 * k, 0)
        decays(2 * k + 2, 0)
        interactions(2 * k + 1, 1)
        return carry

    decays(0, 0)
    lax.fori_loop(0, n_cpairs // 2 - 1, pipelined, 0)
    decays(n_cpairs - 1, 1)
    interactions(n_cpairs - 2, 0)
    interactions(n_cpairs - 1, 1)
    lax.fori_loop(0, n_chunks, values, 0, unroll=6)

    st_ref[...] = jnp.zeros(st_ref.shape, F32)

    def inter(i, carry):
        cb = jnp.where(i < n_ctx_chunks, n_ctx_chunks - 1 - i, n_chunks - 1 - (i - n_ctx_chunks))
        for d, ci in ((0, i), (1, cb)):
            rows = pl.ds(pl.multiple_of(ci * c, c), c)
            st = st_ref[d]
            o_ref[d, rows, :] += _dot_nt(qx_ref[d, rows, :], st.astype(BF16))
            st_ref[d] = st * dec_ref[d, ci][0:1] + u_ref[d, ci]
        return carry

    lax.fori_loop(0, n_chunks, inter, 0, unroll=6)

    nw = nw_ref[...]

    def finish(j, carry):
        rows = pl.ds(pl.multiple_of(j * TOKEN_TILE, TOKEN_TILE), TOKEN_TILE)
        o = o_ref[0, rows, :] + o_ref[1, rows, :]
        o = o * lax.rsqrt(jnp.mean(o * o, axis=-1, keepdims=True) + RMS_EPS)
        gate = _silu(g_ref[0, rows, :])
        y_ref[0, rows, :] = (o * nw * gate).astype(y_ref.dtype)
        return carry

    lax.fori_loop(0, t_all // TOKEN_TILE, finish, 0)


def _hgrn_scan(proj, lb_logits, norm_w_all, j_w, *, layer, n_ctx):
    b, t, n = proj.shape
    d = n // 5
    hd = HGRN_HEAD_DIM
    heads = d // hd
    n_chunks = t // HGRN_CHUNK
    assert n_chunks % 4 == 0
    dmat, mask = _hgrn_constants()
    part = lambda p: pl.BlockSpec((1, t, hd), lambda i, h: (i, 0, p * heads + h))
    kern = functools.partial(_hgrn_scan_kernel, layer=layer, n_ctx=n_ctx)
    return pl.pallas_call(
        kern,
        grid=(b, heads),
        in_specs=[
            part(0), part(1), part(2), part(3), part(4),
            pl.BlockSpec((2, lb_logits.shape[1], hd), lambda i, h: (0, 0, h)),
            pl.BlockSpec((None, 1, hd), lambda i, h: (j_w, 0, h)),
            _resident(dmat.shape),
            _resident(mask.shape),
        ],
        out_specs=pl.BlockSpec((1, t, hd), lambda i, h: (i, 0, h)),
        out_shape=jax.ShapeDtypeStruct((b, t, d), BF16),
        scratch_shapes=[
            pltpu.VMEM((2, t, hd), F32),
            pltpu.VMEM((2, t, hd), BF16),
            pltpu.VMEM((2, t, hd), BF16),
            pltpu.VMEM((2, t, hd), BF16),
            pltpu.VMEM((2, 2 * HGRN_CHUNK, hd), F32),
            pltpu.VMEM((2, 2, 2 * HGRN_CHUNK, hd), F32),
            pltpu.VMEM((2, 2, 2, (dmat.shape[1] // HGRN_CHUNK - 2) * HGRN_CHUNK, hd), F32),
            pltpu.VMEM((2, n_chunks, hd, hd), F32),
            pltpu.VMEM((2, n_chunks, 8, hd), F32),
            pltpu.VMEM((2, hd, hd), F32),
        ],
        compiler_params=pltpu.CompilerParams(
            dimension_semantics=("arbitrary", "arbitrary"), vmem_limit_bytes=VMEM_LIMIT),
        name="hgrn_scan",
    )(proj, proj, proj, proj, proj, lb_logits, norm_w_all.reshape(norm_w_all.shape[0], 1, d),
      jnp.asarray(dmat, BF16), jnp.asarray(mask, F32))


def _na_bias_table(rpb):
    qcol = np.arange(GRID_W)[:, None]
    kcol = np.arange(GRID_W)[None, :]
    win = np.clip(qcol - NA_COLS // 2, 0, GRID_W - NA_COLS)
    ok = (kcol >= win) & (kcol < win + NA_COLS)
    dcol = np.arange(2 * NA_COLS - 1)[:, None, None]
    select = ((kcol - qcol + NA_COLS - 1)[None] == dcol) & ok[None]
    per_row = jnp.einsum('hrc,cqk->hrqk', rpb.astype(F32), jnp.asarray(select, F32),
                         precision=lax.Precision.HIGHEST)
    per_row = jnp.where(jnp.asarray(ok)[None, None], per_row * LOG2E, NEG_BIG)
    return jnp.concatenate([per_row[:, :-1], per_row[:, 1:]], axis=-1)


def _na_kernel(q_ref, k_ref, v_ref, bias_ref, y_ref, qs_ref, s_ref, p_ref, l_ref, *, n_ctx, rows):
    w = GRID_W
    kr = min(NA_ROWS, rows)
    n_loc = kr * w
    grp = NA_ROW_GROUP
    lanes = 2 * NA_HEAD_DIM
    first = lax.broadcasted_iota(jnp.int32, (1, lanes), 1) < NA_HEAD_DIM

    def split_heads(qv):
        zero = jnp.zeros_like(qv)
        return jnp.concatenate([jnp.where(first, qv, zero), jnp.where(first, zero, qv)], axis=0)

    def merge_heads(o, n):
        return jnp.where(first, o[:n], o[n:])

    def window(r):
        rs = jnp.clip(r - kr // 2, 0, rows - kr)
        return r - rs, pl.ds(pl.multiple_of(n_ctx + rs * w, w), n_loc)

    kc = k_ref[0, 0:n_ctx, :]
    vc = v_ref[0, 0:n_ctx, :]

    s = _dot_nt(split_heads(q_ref[0, 0:n_ctx, :]), kc)
    p = jnp.exp2(s - jnp.max(s, axis=-1, keepdims=True))
    o = _dot(p.astype(BF16), vc) / jnp.sum(p, axis=-1, keepdims=True)
    y_ref[0, 0:n_ctx, :] = merge_heads(o, n_ctx).astype(y_ref.dtype)

    def prep(r, carry):
        qs_ref[r] = split_heads(q_ref[0, pl.ds(pl.multiple_of(n_ctx + r * w, w), w), :])
        return carry

    def scores(g, slot):
        for rr in range(grp):
            case, keys = window(g * grp + rr)
            head_bias = lambda hh: jnp.concatenate(
                [bias_ref[hh, NA_ROWS - 1 - case + a] for a in range(0, kr, 2)], axis=1)
            bias = jnp.concatenate([head_bias(0), head_bias(1)], axis=0)
            s_ref[slot, rr, :, 0:n_loc] = _dot_nt(qs_ref[g * grp + rr], k_ref[0, keys, :]) + bias
        qg = qs_ref[pl.ds(g * grp, grp)].reshape(grp * 2 * w, lanes)
        s_ref[slot, :, :, n_loc:] = _dot_nt(qg, kc).reshape(grp, 2 * w, n_ctx)

    def softmax(slot):
        for rr in range(grp):
            sr = s_ref[slot, rr]
            tiles = [sr[:, i * lanes:(i + 1) * lanes] for i in range(sr.shape[1] // lanes)]
            m = jnp.max(functools.reduce(jnp.maximum, tiles), axis=-1, keepdims=True)
            pr = [jnp.exp2(tile - m) for tile in tiles]
            den = jnp.sum(functools.reduce(lambda u, v_: u + v_, pr), axis=-1, keepdims=True)
            l_ref[slot, rr] = jnp.broadcast_to(den, (2 * w, lanes))
            p_ref[slot, rr] = jnp.concatenate(pr, axis=1).astype(BF16)

    def values(g, slot):
        pg = p_ref[slot, :, :, n_loc:].reshape(grp * 2 * w, n_ctx)
        o_ctx = _dot(pg, vc).reshape(grp, 2 * w, lanes)
        for rr in range(grp):
            _, keys = window(g * grp + rr)
            orow = (_dot(p_ref[slot, rr, :, 0:n_loc], v_ref[0, keys, :]) + o_ctx[rr]) / l_ref[slot, rr]
            out_rows = pl.ds(pl.multiple_of(n_ctx + (g * grp + rr) * w, w), w)
            y_ref[0, out_rows, :] = merge_heads(orow, w).astype(y_ref.dtype)

    n_groups = rows // grp

    def pipelined(k, carry):
        g = 2 * k
        scores(g + 2, 0)
        softmax(1)
        values(g, 0)
        scores(g + 3, 1)
        softmax(0)
        values(g + 1, 1)
        return carry

    lax.fori_loop(0, rows, prep, 0, unroll=8)
    scores(0, 0)
    scores(1, 1)
    softmax(0)
    lax.fori_loop(0, n_groups // 2 - 1, pipelined, 0)
    softmax(1)
    values(n_groups - 2, 0)
    values(n_groups - 1, 1)


def _na_attention(qkv, bias, *, n_ctx):
    b, t, n = qkv.shape
    d = n // 3
    lanes = 2 * NA_HEAD_DIM
    pairs = d // lanes
    rows = (t - n_ctx) // GRID_W
    assert rows % NA_ROW_GROUP == 0 and rows >= NA_ROWS and NA_ROWS % 2 == 0
    n_keys = NA_ROWS * GRID_W + n_ctx
    part = lambda p: pl.BlockSpec((1, t, lanes), lambda h, i: (i, 0, p * pairs + h))
    kern = functools.partial(_na_kernel, n_ctx=n_ctx, rows=rows)
    return pl.pallas_call(
        kern,
        grid=(pairs, b),
        in_specs=[
            part(0), part(1), part(2),
            pl.BlockSpec((2,) + bias.shape[1:], lambda h, i: (h, 0, 0, 0)),
        ],
        out_specs=pl.BlockSpec((1, t, lanes), lambda h, i: (i, 0, h)),
        out_shape=jax.ShapeDtypeStruct((b, t, d), BF16),
        scratch_shapes=[
            pltpu.VMEM((rows, 2 * GRID_W, lanes), BF16),
            pltpu.VMEM((2, NA_ROW_GROUP, 2 * GRID_W, n_keys), F32),
            pltpu.VMEM((2, NA_ROW_GROUP, 2 * GRID_W, n_keys), BF16),
            pltpu.VMEM((2, NA_ROW_GROUP, 2 * GRID_W, lanes), F32),
        ],
        compiler_params=pltpu.CompilerParams(
            dimension_semantics=("arbitrary", "arbitrary"), vmem_limit_bytes=VMEM_LIMIT),
        name="na_attention",
    )(qkv, qkv, qkv, bias)


def kernel(x, c, ctx, c_ctx, mod_w, mod_b, ln_g, ln_b, ffn_w_in, ffn_w_out, pool_w, pool_scale,
           hgrn_w_in, hgrn_lb_logits, hgrn_norm_w, hgrn_w_out, na_w_qkv, na_rpb, na_w_out):
    b, seq, d = x.shape
    n_ctx = ctx.shape[1]
    depth = mod_w.shape[0]
    assert n_ctx % TOKEN_TILE == 0 and seq % TOKEN_TILE == 0 and b < MOD_ROWS
    assert seq % GRID_W == 0 and n_ctx % HGRN_CHUNK == 0 and seq % HGRN_CHUNK == 0

    cc = jnp.concatenate([c, c_ctx[None, :], jnp.zeros((MOD_ROWS - b - 1, d), c.dtype)], axis=0)
    mod_all = _modulation(cc, mod_w, mod_b).reshape(depth, MOD_ROWS, N_MOD, d)
    w_in_all = ffn_w_in.astype(BF16)
    w_out_all = ffn_w_out.astype(BF16)
    hgrn_w_in_all = hgrn_w_in.astype(BF16)
    hgrn_w_out_all = hgrn_w_out.astype(BF16)
    na_w_qkv_all = na_w_qkv.astype(BF16)
    na_w_out_all = na_w_out.astype(BF16)
    n_ctx_tiles = n_ctx // TOKEN_TILE

    for i in range(depth):
        last = i == depth - 1
        kind = i % N_MIXERS
        j = i // N_MIXERS
        w_mix_all = None
        y_tile_offset = n_ctx_tiles if last else 0
        if kind == 0:
            y = _pool_mixer(None if last else ctx, x, mod_all, i, pool_w, pool_scale, j)
            y_tile_offset = 0
        elif kind == 1:
            proj = _modulated_proj(ctx, x, mod_all, i, hgrn_w_in_all, j, F32, n_chunk=d)
            y = _hgrn_scan(proj, hgrn_lb_logits, hgrn_norm_w, j, layer=i, n_ctx=n_ctx)
            w_mix_all = hgrn_w_out_all
        else:
            qkv = _modulated_proj(ctx, x, mod_all, i, na_w_qkv_all, j, BF16, n_chunk=d,
                                  first_chunk_scale=NA_HEAD_DIM ** -0.5 * LOG2E)
            y = _na_attention(qkv, _na_bias_table(na_rpb[j]), n_ctx=n_ctx)
            w_mix_all = na_w_out_all
        ctx, x = _ffn_block(None if last else ctx, x, y, y_tile_offset, mod_all, i, ln_g, ln_b,
                            w_mix_all, j, w_in_all, w_out_all)
    return x
```

```python
import functools

import numpy as np
import jax
import jax.numpy as jnp
from jax import lax
from jax.experimental import pallas as pl
from jax.experimental.pallas import tpu as pltpu

DEPTH = 4
N_MIXERS = 3
N_MOD = 6
ALPHA = (2 * DEPTH) ** 0.25
LN_EPS = 1e-6
RMS_EPS = 1e-6
POOL_WINDOWS = (2, 4, 8, 16)
HGRN_HEAD_DIM = 128
HGRN_CHUNK = 64
NA_HEAD_DIM = 64
NA_ROWS = 8
NA_COLS = 16
GRID_W = 64
NEG_BIG = -1e30
LOG2E = 1.4426950408889634
NA_ROW_GROUP = 4

TOKEN_TILE = 256
FFN_STEP_TILES = 2
MOD_ROWS = 16
VMEM_LIMIT = 56 * 1024 * 1024

F32 = jnp.float32
BF16 = jnp.bfloat16


def _sigmoid(x):
    return 1.0 / (1.0 + jnp.exp(-x))


def _silu(x):
    return x * _sigmoid(x)


def _layer_norm(v, g, b):
    mu = jnp.mean(v, axis=-1, keepdims=True)
    d = v - mu
    var = jnp.mean(d * d, axis=-1, keepdims=True)
    return d * lax.rsqrt(var + LN_EPS) * g + b


def _dot(a, b):
    return jnp.dot(a, b, preferred_element_type=F32)


def _dot_nt(a, b):
    return lax.dot_general(a, b, (((1,), (1,)), ((), ())), preferred_element_type=F32)


def _dot_tn(a, b):
    return lax.dot_general(a, b, (((0,), (0,)), ((), ())), preferred_element_type=F32)


def _resident(shape, layer=None):
    zeros = (0,) * len(shape)
    if layer is None:
        return pl.BlockSpec(shape, lambda *_: zeros, pipeline_mode=pl.Buffered(1))
    return pl.BlockSpec((None,) + tuple(shape), lambda *_: (layer,) + zeros, pipeline_mode=pl.Buffered(1))


def _mod_kernel(c_ref, w_ref, b_ref, o_ref):
    s = _silu(c_ref[...]).astype(BF16)
    o_ref[0] = _dot(s, w_ref[0].astype(BF16)) + b_ref[0]


def _modulation(cc, mod_w, mod_b):
    depth, d, n = mod_w.shape
    tn = n // 4
    return pl.pallas_call(
        _mod_kernel,
        grid=(depth, n // tn),
        in_specs=[
            pl.BlockSpec((MOD_ROWS, d), lambda i, j: (0, 0)),
            pl.BlockSpec((1, d, tn), lambda i, j: (i, 0, j)),
            pl.BlockSpec((1, 1, tn), lambda i, j: (i, 0, j)),
        ],
        out_specs=pl.BlockSpec((1, MOD_ROWS, tn), lambda i, j: (i, 0, j)),
        out_shape=jax.ShapeDtypeStruct((depth, MOD_ROWS, n), F32),
        compiler_params=pltpu.CompilerParams(
            dimension_semantics=("arbitrary", "arbitrary"), vmem_limit_bytes=VMEM_LIMIT),
        name="modulation",
    )(cc, mod_w, mod_b.reshape(depth, 1, n))


def _pool_kernel(*refs, with_ctx, halo):
    if with_ctx:
        c_ref, x_ref, modc_ref, modx_ref, w_ref, s_ref, oc_ref, ox_ref, pad_ref = refs
    else:
        x_ref, modx_ref, w_ref, s_ref, ox_ref, pad_ref = refs
    g = pl.program_id(1)
    wg = w_ref[0].astype(BF16)
    scale = s_ref[...]

    def segment(src_ref, mod_ref, dst_ref, window):
        length = src_ref.shape[1]
        lo = window // 2
        hi = window - 1 - lo
        h = src_ref[0] * (1.0 + mod_ref[0, 1:2, :]) + mod_ref[0, 0:1, :]
        zeros = jnp.zeros((halo, h.shape[1]), F32)
        pad_ref[0:halo, :] = zeros
        pad_ref[halo:halo + length, :] = h
        pad_ref[halo + length:2 * halo + length, :] = zeros
        tot = h
        for off in range(-lo, hi + 1):
            if off != 0:
                tot = tot + pad_ref[halo + off:halo + off + length, :]
        t = lax.broadcasted_iota(jnp.int32, h.shape, 0)
        cnt = jnp.minimum(t + hi + 1, length) - jnp.maximum(t - lo, 0)
        p = tot / cnt.astype(F32) - h
        dst_ref[0] = _dot(p.astype(BF16), wg) * scale

    for gi, window in enumerate(POOL_WINDOWS):
        @pl.when(g == gi)
        def _(window=window):
            if with_ctx:
                segment(c_ref, modc_ref, oc_ref, window)
            segment(x_ref, modx_ref, ox_ref, window)


def _pool_mixer(ctx, x, mod_all, layer, w_groups, scales, j):
    b, seq, d = x.shape
    _, ng, gd, _ = w_groups.shape
    halo = 8
    with_ctx = ctx is not None
    n_ctx = ctx.shape[1] if with_ctx else 0
    mod_spec = lambda row: pl.BlockSpec((None, 1, N_MOD, gd), lambda i, g: (layer, row(i), 0, g))
    in_specs, args = [], []
    if with_ctx:
        in_specs.append(pl.BlockSpec((1, n_ctx, gd), lambda i, g: (i, 0, g)))
        args.append(ctx)
    in_specs.append(pl.BlockSpec((1, seq, gd), lambda i, g: (i, 0, g)))
    args.append(x)
    if with_ctx:
        in_specs.append(mod_spec(lambda i: b))
        args.append(mod_all)
    in_specs += [
        mod_spec(lambda i: i),
        pl.BlockSpec((None, 1, gd, gd), lambda i, g: (j, g, 0, 0)),
        pl.BlockSpec((None, 1, gd), lambda i, g: (j, 0, g)),
    ]
    args += [mod_all, w_groups, scales.reshape(scales.shape[0], 1, d)]
    out_specs = [pl.BlockSpec((1, seq, gd), lambda i, g: (i, 0, g))]
    out_shape = [jax.ShapeDtypeStruct((b, seq, d), F32)]
    if with_ctx:
        out_specs.insert(0, pl.BlockSpec((1, n_ctx, gd), lambda i, g: (i, 0, g)))
        out_shape.insert(0, jax.ShapeDtypeStruct((b, n_ctx, d), F32))
    kern = functools.partial(_pool_kernel, with_ctx=with_ctx, halo=halo)
    outs = pl.pallas_call(
        kern,
        grid=(b, ng),
        in_specs=in_specs,
        out_specs=out_specs,
        out_shape=out_shape,
        scratch_shapes=[pltpu.VMEM((seq + 2 * halo, gd), F32)],
        compiler_params=pltpu.CompilerParams(
            dimension_semantics=("arbitrary", "arbitrary"), vmem_limit_bytes=VMEM_LIMIT),
        name="pool_mixer",
    )(*args)
    return (outs[0], outs[1]) if with_ctx else (None, outs[0])


def _proj_kernel(c_ref, x_ref, mod_ref, w_ref, o_ref, *, n_ctx_tiles, n_chunk, first_chunk_scale):
    def run(src_ref):
        h = (src_ref[0] * (1.0 + mod_ref[0, 1:2, :]) + mod_ref[0, 0:1, :]).astype(BF16)
        n = w_ref.shape[1]
        for c0 in range(0, n, n_chunk):
            r = _dot(h, w_ref[:, c0:c0 + n_chunk])
            if c0 == 0 and first_chunk_scale != 1.0:
                r = r * first_chunk_scale
            o_ref[0, :, c0:c0 + n_chunk] = r.astype(o_ref.dtype)

    _per_segment(pl.program_id(1), n_ctx_tiles, lambda: run(c_ref), lambda: run(x_ref))


def _per_segment(j, n_ctx_tiles, ctx_fn, latent_fn):
    pl.when(j < n_ctx_tiles)(ctx_fn)
    pl.when(j >= n_ctx_tiles)(latent_fn)


def _segment_specs(b, d, n_ctx_tiles, layer):
    return [
        pl.BlockSpec((1, TOKEN_TILE, d), lambda i, j: (i, jnp.minimum(j, n_ctx_tiles - 1), 0)),
        pl.BlockSpec((1, TOKEN_TILE, d), lambda i, j: (i, jnp.maximum(j - n_ctx_tiles, 0), 0)),
        pl.BlockSpec((None, 1, N_MOD, d), lambda i, j: (layer, jnp.where(j < n_ctx_tiles, b, i), 0, 0)),
    ]


def _modulated_proj(ctx, x, mod_all, layer, w_all, j_w, out_dtype, n_chunk, first_chunk_scale=1.0):
    b, seq, d = x.shape
    n_ctx = ctx.shape[1]
    n = w_all.shape[2]
    nct = n_ctx // TOKEN_TILE
    kern = functools.partial(_proj_kernel, n_ctx_tiles=nct, n_chunk=n_chunk,
                             first_chunk_scale=first_chunk_scale)
    return pl.pallas_call(
        kern,
        grid=(b, (n_ctx + seq) // TOKEN_TILE),
        in_specs=_segment_specs(b, d, nct, layer) + [_resident((d, n), j_w)],
        out_specs=pl.BlockSpec((1, TOKEN_TILE, n), lambda i, j: (i, j, 0)),
        out_shape=jax.ShapeDtypeStruct((b, n_ctx + seq, n), out_dtype),
        compiler_params=pltpu.CompilerParams(
            dimension_semantics=("arbitrary", "arbitrary"), vmem_limit_bytes=VMEM_LIMIT),
        name="modulated_proj",
    )(ctx, x, mod_all, w_all)


def _ffn_kernel(*refs, with_ctx, n_ctx_tiles, has_mix, ffn_hidden):
    refs = list(refs)
    if with_ctx:
        c_ref, x_ref, yc_ref, yx_ref = refs[:4]
        refs = refs[4:]
    else:
        x_ref, yx_ref = refs[:2]
        refs = refs[2:]
    mod_ref, lng_ref, lnb_ref = refs[:3]
    wmix_ref = refs[3] if has_mix else None
    win_ref, wout_ref = refs[3 + has_mix:5 + has_mix]
    outs = refs[5 + has_mix:]

    def run(src_ref, y_ref, dst_ref):
        g1 = mod_ref[0, 2:3, :]
        sh2 = mod_ref[0, 3:4, :]
        sc2 = mod_ref[0, 4:5, :]
        g2 = mod_ref[0, 5:6, :]
        for r0 in range(0, src_ref.shape[1], TOKEN_TILE):
            rows = slice(r0, r0 + TOKEN_TILE)
            x = src_ref[0, rows, :]
            if has_mix:
                ox = _dot(y_ref[0, rows, :], wmix_ref[...])
            else:
                ox = y_ref[0, rows, :]
            x1 = _layer_norm(ALPHA * x + g1 * ox, lng_ref[0:1, :], lnb_ref[0:1, :])
            h2 = (x1 * (1.0 + sc2) + sh2).astype(BF16)
            gu = _dot(h2, win_ref[...])
            act = (_silu(gu[:, :ffn_hidden]) * gu[:, ffn_hidden:]).astype(BF16)
            y2 = _dot(act, wout_ref[...])
            dst_ref[0, rows, :] = _layer_norm(ALPHA * x1 + g2 * y2, lng_ref[1:2, :], lnb_ref[1:2, :])

    if with_ctx:
        _per_segment(pl.program_id(1), n_ctx_tiles,
                     lambda: run(c_ref, yc_ref, outs[0]), lambda: run(x_ref, yx_ref, outs[1]))
    else:
        run(x_ref, yx_ref, outs[0])


def _ffn_block(ctx, x, y_ctx, y_x, mod_all, layer, ln_g, ln_b, w_mix_all, j_mix, w_in_all, w_out_all):
    b, seq, d = x.shape
    f = w_out_all.shape[1]
    with_ctx = ctx is not None
    nct = ctx.shape[1] // TOKEN_TILE if with_ctx else 0
    has_mix = w_mix_all is not None
    step = FFN_STEP_TILES * TOKEN_TILE
    assert seq % step == 0
    c_map = lambda i, j: (i, jnp.minimum(j, nct - 1), 0)
    x_map = lambda i, j: (i, jnp.maximum(j - nct, 0), 0)
    c_spec = pl.BlockSpec((1, TOKEN_TILE, d), c_map)
    x_spec = pl.BlockSpec((1, step, d), x_map)
    if with_ctx:
        in_specs = [c_spec, x_spec, c_spec, x_spec,
                    pl.BlockSpec((None, 1, N_MOD, d), lambda i, j: (layer, jnp.where(j < nct, b, i), 0, 0))]
        args = [ctx, x, y_ctx, y_x, mod_all]
        out_specs = [c_spec, x_spec]
        out_shape = [jax.ShapeDtypeStruct(ctx.shape, F32), jax.ShapeDtypeStruct(x.shape, F32)]
    else:
        in_specs = [x_spec, x_spec, pl.BlockSpec((None, 1, N_MOD, d), lambda i, j: (layer, i, 0, 0))]
        args = [x, y_x, mod_all]
        out_specs = [x_spec]
        out_shape = [jax.ShapeDtypeStruct(x.shape, F32)]
    in_specs += [_resident((2, d), layer), _resident((2, d), layer)]
    args += [ln_g, ln_b]
    if has_mix:
        in_specs.append(_resident((d, d), j_mix))
        args.append(w_mix_all)
    in_specs += [_resident((d, 2 * f), layer), _resident((f, d), layer)]
    args += [w_in_all, w_out_all]
    kern = functools.partial(_ffn_kernel, with_ctx=with_ctx, n_ctx_tiles=nct, has_mix=has_mix, ffn_hidden=f)
    outs = pl.pallas_call(
        kern,
        grid=(b, nct + seq // step),
        in_specs=in_specs,
        out_specs=out_specs,
        out_shape=out_shape,
        compiler_params=pltpu.CompilerParams(
            dimension_semantics=("arbitrary", "arbitrary"), vmem_limit_bytes=VMEM_LIMIT),
        name="ffn_block",
    )(*args)
    return (outs[0], outs[1]) if with_ctx else (None, outs[0])


def _hgrn_constants():
    c = HGRN_CHUNK
    t = np.arange(c)[:, None]
    u = np.arange(c)[None, :]
    mats = [u <= t]
    masks = []
    m = c // 2
    while m >= 1:
        ref = (t // (2 * m)) * 2 * m + m - 1
        second = (t % (2 * m)) >= m
        mats.append(np.where(second, (u > ref) & (u <= t), (u > t) & (u <= ref)))
        masks.append(((t // (2 * m)) == (u // (2 * m))) & second & ((u % (2 * m)) < m))
        m //= 2
    mats.append(u > t)
    masks += [np.broadcast_to(t == u, (c, c)), np.zeros((c, c), bool)]
    pair = lambda ms: np.stack([np.concatenate([ms[2 * p], ms[2 * p + 1]], 1) for p in range(len(ms) // 2)])
    flip = lambda ms: [a[::-1, ::-1] for a in ms]
    dmat = np.stack([np.concatenate(mats, 0), np.concatenate(flip(mats), 0)])
    dmat = np.concatenate([dmat] * 3, axis=2)
    mask = np.stack([pair(masks), pair(flip(masks))])
    return dmat.astype(np.float32), mask.astype(np.float32)


def _hgrn_scan_kernel(q_ref, v_ref, zf_ref, zb_ref, g_ref, lbl_ref, nw_ref, dm_ref, mk_ref,
                      yc_ref, yx_ref, o_ref, oi_ref, qx_ref, ke_ref, r_ref, qf_ref, kf_ref, x_ref, u_ref, dec_ref,
                      *, layer, n_ctx):
    c = HGRN_CHUNK
    hd = HGRN_HEAD_DIM
    t_all = q_ref.shape[1]
    n_chunks = t_all // c
    n_ctx_chunks = n_ctx // c
    n_pairs = mk_ref.shape[1]
    n_blocks = dm_ref.shape[1] // c

    def lower_bound(direction):
        rows = [lbl_ref[direction, j:j + 1, :] for j in range(lbl_ref.shape[1])]
        mx = functools.reduce(jnp.maximum, rows)
        e = [jnp.exp(r - mx) for r in rows]
        tot = functools.reduce(lambda a, b_: a + b_, e)
        p = [ei / tot for ei in e]
        acc = p[0]
        for j in range(1, layer + 1):
            acc = acc + p[j]
        return acc - p[0]

    lb = (lower_bound(0), lower_bound(1))
    z_refs = (zf_ref, zb_ref)
    end_row = (c - 1, 0)
    zeros_k = jnp.zeros((c, hd), BF16)

    def decays(i, slot):
        rows2 = pl.ds(pl.multiple_of(i * 2 * c, 2 * c), 2 * c)
        q = _silu(q_ref[0, rows2, :])
        qf_ref[slot] = q.astype(BF16)
        for d in range(2):
            f = lb[d] + (1.0 - lb[d]) * _sigmoid(z_refs[d][0, rows2, :])
            k = 1.0 - f
            kf_ref[slot, d] = k.astype(BF16)
            g = jnp.log(f) * LOG2E
            hi = g.astype(BF16)
            r1 = g - hi.astype(F32)
            mid = r1.astype(BF16)
            lo = (r1 - mid.astype(F32)).astype(BF16)
            side = lambda a: jnp.concatenate([a[:c], a[c:]], axis=1)
            x2 = jnp.exp2(_dot(dm_ref[d], jnp.concatenate([side(hi), side(mid), side(lo)], axis=0)))
            for cc in range(2):
                xs = x2[:, cc * hd:(cc + 1) * hd]
                rows = pl.ds(pl.multiple_of(i * 2 * c + cc * c, c), c)
                x_ref[slot, d, cc] = xs[c:(n_blocks - 1) * c].astype(BF16)
                qx_ref[d, rows, :] = (q[cc * c:(cc + 1) * c] * xs[0:c]).astype(BF16)
                ke_ref[d, rows, :] = (k[cc * c:(cc + 1) * c] * xs[(n_blocks - 1) * c:]).astype(BF16)
                dec_ref[d, 2 * i + cc] = jnp.broadcast_to(xs[end_row[d]:end_row[d] + 1], (8, hd))

    def interactions(i, slot):
        for cc in range(2):
            rows = pl.ds(pl.multiple_of(i * 2 * c + cc * c, c), c)
            qc = qf_ref[slot, cc * c:(cc + 1) * c, :]
            for d in range(2):
                kc = kf_ref[slot, d, cc * c:(cc + 1) * c, :]
                xs = x_ref[slot, d, cc]
                diag = jnp.sum(qc.astype(F32) * kc.astype(F32), axis=-1, keepdims=True)
                r = mk_ref[d, n_pairs - 1] * diag
                for p in range(n_pairs - 1):
                    xa = xs[2 * p * c:(2 * p + 1) * c]
                    xb = xs[(2 * p + 1) * c:(2 * p + 2) * c]
                    lhs = jnp.concatenate([qc * xa, qc * xb], axis=1)
                    rhs = jnp.concatenate([jnp.concatenate([kc * xa, zeros_k], axis=1),
                                           jnp.concatenate([zeros_k, kc * xb], axis=1)], axis=0)
                    r = r + mk_ref[d, p] * _dot_nt(lhs, rhs)
                r_ref[d, rows, :] = r.astype(BF16)

    def values(ci, carry):
        rows = pl.ds(pl.multiple_of(ci * c, c), c)
        vc = v_ref[0, rows, :].astype(BF16)
        oo = _dot(jnp.concatenate([r_ref[0, rows, :], r_ref[1, rows, :]], axis=0),
                  jnp.concatenate([vc, vc], axis=0))
        o_ref[0, rows, :] = oo[:c]
        o_ref[1, rows, :] = oo[c:]
        uu = _dot_tn(vc, jnp.concatenate([ke_ref[0, rows, :], ke_ref[1, rows, :]], axis=1))
        u_ref[0, ci] = uu[:, :hd]
        u_ref[1, ci] = uu[:, hd:]
        return carry

    n_cpairs = n_chunks // 2

    def pipelined(k, carry):
        decays(2 * k + 1, 1)
        interactions(2 * k, 0)
        decays(2 * k + 2, 0)
        interactions(2 * k + 1, 1)
        return carry

    decays(0, 0)
    lax.fori_loop(0, n_cpairs // 2 - 1, pipelined, 0)
    decays(n_cpairs - 1, 1)
    interactions(n_cpairs - 2, 0)
    interactions(n_cpairs - 1, 1)
    lax.fori_loop(0, n_chunks, values, 0, unroll=6)

    def inter(i, states):
        cb = jnp.where(i < n_ctx_chunks, n_ctx_chunks - 1 - i, n_chunks - 1 - (i - n_ctx_chunks))
        new_states = []
        for d, ci in ((0, i), (1, cb)):
            rows = pl.ds(pl.multiple_of(ci * c, c), c)
            st = states[d]
            oi_ref[d, rows, :] = _dot_nt(qx_ref[d, rows, :], st.astype(BF16))
            new_states.append(st * dec_ref[d, ci][0:1] + u_ref[d, ci])
        return tuple(new_states)

    zero_state = jnp.zeros((hd, hd), F32)
    lax.fori_loop(0, n_chunks, inter, (zero_state, zero_state), unroll=6)

    nw = nw_ref[...]

    def finish(dst_ref, first_row):
        def tile(j, carry):
            rows = pl.ds(pl.multiple_of(first_row + j * TOKEN_TILE, TOKEN_TILE), TOKEN_TILE)
            o = (o_ref[0, rows, :] + oi_ref[0, rows, :]) + (o_ref[1, rows, :] + oi_ref[1, rows, :])
            o = o * lax.rsqrt(jnp.mean(o * o, axis=-1, keepdims=True) + RMS_EPS)
            gate = _silu(g_ref[0, rows, :])
            out_rows = pl.ds(pl.multiple_of(j * TOKEN_TILE, TOKEN_TILE), TOKEN_TILE)
            dst_ref[0, out_rows, :] = (o * nw * gate).astype(dst_ref.dtype)
            return carry

        lax.fori_loop(0, dst_ref.shape[1] // TOKEN_TILE, tile, 0)

    finish(yc_ref, 0)
    finish(yx_ref, n_ctx)


def _hgrn_scan(proj, lb_logits, norm_w_all, j_w, *, layer, n_ctx):
    b, t, n = proj.shape
    d = n // 5
    hd = HGRN_HEAD_DIM
    heads = d // hd
    n_chunks = t // HGRN_CHUNK
    assert n_chunks % 4 == 0
    dmat, mask = _hgrn_constants()
    part = lambda p: pl.BlockSpec((1, t, hd), lambda i, h: (i, 0, p * heads + h))
    kern = functools.partial(_hgrn_scan_kernel, layer=layer, n_ctx=n_ctx)
    return pl.pallas_call(
        kern,
        grid=(b, heads),
        in_specs=[
            part(0), part(1), part(2), part(3), part(4),
            pl.BlockSpec((2, lb_logits.shape[1], hd), lambda i, h: (0, 0, h)),
            pl.BlockSpec((None, 1, hd), lambda i, h: (j_w, 0, h)),
            _resident(dmat.shape),
            _resident(mask.shape),
        ],
        out_specs=[pl.BlockSpec((1, n_ctx, hd), lambda i, h: (i, 0, h)),
                   pl.BlockSpec((1, t - n_ctx, hd), lambda i, h: (i, 0, h))],
        out_shape=[jax.ShapeDtypeStruct((b, n_ctx, d), BF16),
                   jax.ShapeDtypeStruct((b, t - n_ctx, d), BF16)],
        scratch_shapes=[
            pltpu.VMEM((2, t, hd), F32),
            pltpu.VMEM((2, t, hd), F32),
            pltpu.VMEM((2, t, hd), BF16),
            pltpu.VMEM((2, t, hd), BF16),
            pltpu.VMEM((2, t, hd), BF16),
            pltpu.VMEM((2, 2 * HGRN_CHUNK, hd), BF16),
            pltpu.VMEM((2, 2, 2 * HGRN_CHUNK, hd), BF16),
            pltpu.VMEM((2, 2, 2, (dmat.shape[1] // HGRN_CHUNK - 2) * HGRN_CHUNK, hd), BF16),
            pltpu.VMEM((2, n_chunks, hd, hd), F32),
            pltpu.VMEM((2, n_chunks, 8, hd), F32),
        ],
        compiler_params=pltpu.CompilerParams(
            dimension_semantics=("arbitrary", "arbitrary"), vmem_limit_bytes=VMEM_LIMIT),
        name="hgrn_scan",
    )(proj, proj, proj, proj, proj, lb_logits, norm_w_all.reshape(norm_w_all.shape[0], 1, d),
      jnp.asarray(dmat, BF16), jnp.asarray(mask, F32))


def _na_bias_table(rpb):
    qcol = np.arange(GRID_W)[:, None]
    kcol = np.arange(GRID_W)[None, :]
    win = np.clip(qcol - NA_COLS // 2, 0, GRID_W - NA_COLS)
    ok = (kcol >= win) & (kcol < win + NA_COLS)
    dcol = np.arange(2 * NA_COLS - 1)[:, None, None]
    select = ((kcol - qcol + NA_COLS - 1)[None] == dcol) & ok[None]
    per_row = jnp.einsum('hrc,cqk->hrqk', rpb.astype(F32), jnp.asarray(select, F32),
                         precision=lax.Precision.HIGHEST)
    per_row = jnp.where(jnp.asarray(ok)[None, None], per_row * LOG2E, NEG_BIG)
    return jnp.concatenate([per_row[:, :-1], per_row[:, 1:]], axis=-1)


def _na_kernel(q_ref, k_ref, v_ref, bias_ref, yc_ref, yx_ref, qs_ref, s_ref, p_ref, l_ref, *, n_ctx, rows):
    w = GRID_W
    kr = min(NA_ROWS, rows)
    n_loc = kr * w
    grp = NA_ROW_GROUP
    lanes = 2 * NA_HEAD_DIM
    first = lax.broadcasted_iota(jnp.int32, (1, lanes), 1) < NA_HEAD_DIM

    def split_heads(qv):
        zero = jnp.zeros_like(qv)
        return jnp.concatenate([jnp.where(first, qv, zero), jnp.where(first, zero, qv)], axis=0)

    def merge_heads(o, n):
        return jnp.where(first, o[:n], o[n:])

    def window(r):
        rs = jnp.clip(r - kr // 2, 0, rows - kr)
        return r - rs, pl.ds(pl.multiple_of(n_ctx + rs * w, w), n_loc)

    kc = k_ref[0, 0:n_ctx, :]
    vc = v_ref[0, 0:n_ctx, :]

    s = _dot_nt(split_heads(q_ref[0, 0:n_ctx, :]), kc)
    p = jnp.exp2(s - jnp.max(s, axis=-1, keepdims=True))
    o = _dot(p.astype(BF16), vc) / jnp.sum(p, axis=-1, keepdims=True)
    yc_ref[0] = merge_heads(o, n_ctx).astype(yc_ref.dtype)

    def prep(r, carry):
        qs_ref[r] = split_heads(q_ref[0, pl.ds(pl.multiple_of(n_ctx + r * w, w), w), :])
        return carry

    def scores(g, slot):
        for rr in range(grp):
            case, keys = window(g * grp + rr)
            head_bias = lambda hh: jnp.concatenate(
                [bias_ref[hh, NA_ROWS - 1 - case + a] for a in range(0, kr, 2)], axis=1)
            bias = jnp.concatenate([head_bias(0), head_bias(1)], axis=0)
            s_ref[slot, rr, :, 0:n_loc] = _dot_nt(qs_ref[g * grp + rr], k_ref[0, keys, :]) + bias
        qg = qs_ref[pl.ds(g * grp, grp)].reshape(grp * 2 * w, lanes)
        s_ref[slot, :, :, n_loc:] = _dot_nt(qg, kc).reshape(grp, 2 * w, n_ctx)

    def softmax(slot):
        for rr in range(grp):
            sr = s_ref[slot, rr]
            tiles = [sr[:, i * lanes:(i + 1) * lanes] for i in range(sr.shape[1] // lanes)]
            m = jnp.max(functools.reduce(jnp.maximum, tiles), axis=-1, keepdims=True)
            pr = [jnp.exp2(tile - m) for tile in tiles]
            den = jnp.sum(functools.reduce(lambda u, v_: u + v_, pr), axis=-1, keepdims=True)
            l_ref[slot, rr] = jnp.broadcast_to(den, (2 * w, lanes))
            p_ref[slot, rr] = jnp.concatenate(pr, axis=1).astype(BF16)

    def values(g, slot):
        pg = p_ref[slot, :, :, n_loc:].reshape(grp * 2 * w, n_ctx)
        o_ctx = _dot(pg, vc).reshape(grp, 2 * w, lanes)
        for rr in range(grp):
            _, keys = window(g * grp + rr)
            orow = (_dot(p_ref[slot, rr, :, 0:n_loc], v_ref[0, keys, :]) + o_ctx[rr]) / l_ref[slot, rr]
            out_rows = pl.ds(pl.multiple_of((g * grp + rr) * w, w), w)
            yx_ref[0, out_rows, :] = merge_heads(orow, w).astype(yx_ref.dtype)

    n_groups = rows // grp

    def pipelined(k, carry):
        g = 2 * k
        scores(g + 2, 0)
        softmax(1)
        values(g, 0)
        scores(g + 3, 1)
        softmax(0)
        values(g + 1, 1)
        return carry

    lax.fori_loop(0, rows, prep, 0, unroll=8)
    scores(0, 0)
    scores(1, 1)
    softmax(0)
    lax.fori_loop(0, n_groups // 2 - 1, pipelined, 0)
    softmax(1)
    values(n_groups - 2, 0)
    values(n_groups - 1, 1)


def _na_attention(qkv, bias, *, n_ctx):
    b, t, n = qkv.shape
    d = n // 3
    lanes = 2 * NA_HEAD_DIM
    pairs = d // lanes
    rows = (t - n_ctx) // GRID_W
    assert rows % NA_ROW_GROUP == 0 and rows >= NA_ROWS and NA_ROWS % 2 == 0
    n_keys = NA_ROWS * GRID_W + n_ctx
    part = lambda p: pl.BlockSpec((1, t, lanes), lambda h, i: (i, 0, p * pairs + h))
    kern = functools.partial(_na_kernel, n_ctx=n_ctx, rows=rows)
    return pl.pallas_call(
        kern,
        grid=(pairs, b),
        in_specs=[
            part(0), part(1), part(2),
            pl.BlockSpec((2,) + bias.shape[1:], lambda h, i: (h, 0, 0, 0)),
        ],
        out_specs=[pl.BlockSpec((1, n_ctx, lanes), lambda h, i: (i, 0, h)),
                   pl.BlockSpec((1, t - n_ctx, lanes), lambda h, i: (i, 0, h))],
        out_shape=[jax.ShapeDtypeStruct((b, n_ctx, d), BF16),
                   jax.ShapeDtypeStruct((b, t - n_ctx, d), BF16)],
        scratch_shapes=[
            pltpu.VMEM((rows, 2 * GRID_W, lanes), BF16),
            pltpu.VMEM((2, NA_ROW_GROUP, 2 * GRID_W, n_keys), F32),
            pltpu.VMEM((2, NA_ROW_GROUP, 2 * GRID_W, n_keys), BF16),
            pltpu.VMEM((2, NA_ROW_GROUP, 2 * GRID_W, lanes), F32),
        ],
        compiler_params=pltpu.CompilerParams(
            dimension_semantics=("arbitrary", "arbitrary"), vmem_limit_bytes=VMEM_LIMIT),
        name="na_attention",
    )(qkv, qkv, qkv, bias)


def kernel(x, c, ctx, c_ctx, mod_w, mod_b, ln_g, ln_b, ffn_w_in, ffn_w_out, pool_w, pool_scale,
           hgrn_w_in, hgrn_lb_logits, hgrn_norm_w, hgrn_w_out, na_w_qkv, na_rpb, na_w_out):
    b, seq, d = x.shape
    n_ctx = ctx.shape[1]
    depth = mod_w.shape[0]
    assert n_ctx % TOKEN_TILE == 0 and seq % TOKEN_TILE == 0 and b < MOD_ROWS
    assert seq % GRID_W == 0 and n_ctx % HGRN_CHUNK == 0 and seq % HGRN_CHUNK == 0

    cc = jnp.concatenate([c, c_ctx[None, :], jnp.zeros((MOD_ROWS - b - 1, d), c.dtype)], axis=0)
    mod_all = _modulation(cc, mod_w, mod_b).reshape(depth, MOD_ROWS, N_MOD, d)
    w_in_all = ffn_w_in.astype(BF16)
    w_out_all = ffn_w_out.astype(BF16)
    hgrn_w_in_all = hgrn_w_in.astype(BF16)
    hgrn_w_out_all = hgrn_w_out.astype(BF16)
    na_w_qkv_all = na_w_qkv.astype(BF16)
    na_w_out_all = na_w_out.astype(BF16)

    for i in range(depth):
        last = i == depth - 1
        kind = i % N_MIXERS
        j = i // N_MIXERS
        w_mix_all = None
        if kind == 0:
            y_ctx, y_x = _pool_mixer(None if last else ctx, x, mod_all, i, pool_w, pool_scale, j)
        elif kind == 1:
            proj = _modulated_proj(ctx, x, mod_all, i, hgrn_w_in_all, j, F32, n_chunk=d)
            y_ctx, y_x = _hgrn_scan(proj, hgrn_lb_logits, hgrn_norm_w, j, layer=i, n_ctx=n_ctx)
            w_mix_all = hgrn_w_out_all
        else:
            qkv = _modulated_proj(ctx, x, mod_all, i, na_w_qkv_all, j, BF16, n_chunk=d,
                                  first_chunk_scale=NA_HEAD_DIM ** -0.5 * LOG2E)
            y_ctx, y_x = _na_attention(qkv, _na_bias_table(na_rpb[j]), n_ctx=n_ctx)
            w_mix_all = na_w_out_all
        ctx, x = _ffn_block(None if last else ctx, x, y_ctx, y_x, mod_all, i, ln_g, ln_b,
                            w_mix_all, j, w_in_all, w_out_all)
    return x
```

```python
import functools

import numpy as np
import jax
import jax.numpy as jnp
from jax import lax
from jax.experimental import pallas as pl
from jax.experimental.pallas import tpu as pltpu

DEPTH = 4
N_MIXERS = 3
N_MOD = 6
ALPHA = (2 * DEPTH) ** 0.25
LN_EPS = 1e-6
RMS_EPS = 1e-6
POOL_WINDOWS = (2, 4, 8, 16)
HGRN_HEAD_DIM = 128
HGRN_CHUNK = 64
NA_HEAD_DIM = 64
NA_ROWS = 8
NA_COLS = 16
GRID_W = 64
NEG_BIG = -1e30
LOG2E = 1.4426950408889634
NA_ROW_GROUP = 4

TOKEN_TILE = 256
FFN_STEP_TILES = 2
MOD_ROWS = 16
VMEM_LIMIT = 56 * 1024 * 1024

F32 = jnp.float32
BF16 = jnp.bfloat16


def _sigmoid(x):
    return 1.0 / (1.0 + jnp.exp(-x))


def _silu(x):
    return x * _sigmoid(x)


def _layer_norm(v, g, b):
    mu = jnp.mean(v, axis=-1, keepdims=True)
    d = v - mu
    var = jnp.mean(d * d, axis=-1, keepdims=True)
    return d * lax.rsqrt(var + LN_EPS) * g + b


def _dot(a, b):
    return jnp.dot(a, b, preferred_element_type=F32)


def _dot_nt(a, b):
    return lax.dot_general(a, b, (((1,), (1,)), ((), ())), preferred_element_type=F32)


def _dot_tn(a, b):
    return lax.dot_general(a, b, (((0,), (0,)), ((), ())), preferred_element_type=F32)


def _resident(shape, layer=None):
    zeros = (0,) * len(shape)
    if layer is None:
        return pl.BlockSpec(shape, lambda *_: zeros, pipeline_mode=pl.Buffered(1))
    return pl.BlockSpec((None,) + tuple(shape), lambda *_: (layer,) + zeros, pipeline_mode=pl.Buffered(1))


def _mod_kernel(c_ref, w_ref, b_ref, o_ref):
    s = _silu(c_ref[...]).astype(BF16)
    o_ref[0] = _dot(s, w_ref[0].astype(BF16)) + b_ref[0]


def _modulation(cc, mod_w, mod_b):
    depth, d, n = mod_w.shape
    tn = n // 4
    return pl.pallas_call(
        _mod_kernel,
        grid=(depth, n // tn),
        in_specs=[
            pl.BlockSpec((MOD_ROWS, d), lambda i, j: (0, 0)),
            pl.BlockSpec((1, d, tn), lambda i, j: (i, 0, j)),
            pl.BlockSpec((1, 1, tn), lambda i, j: (i, 0, j)),
        ],
        out_specs=pl.BlockSpec((1, MOD_ROWS, tn), lambda i, j: (i, 0, j)),
        out_shape=jax.ShapeDtypeStruct((depth, MOD_ROWS, n), F32),
        compiler_params=pltpu.CompilerParams(
            dimension_semantics=("arbitrary", "arbitrary"), vmem_limit_bytes=VMEM_LIMIT),
        name="modulation",
    )(cc, mod_w, mod_b.reshape(depth, 1, n))


def _pool_kernel(*refs, with_ctx, halo):
    if with_ctx:
        c_ref, x_ref, modc_ref, modx_ref, w_ref, s_ref, oc_ref, ox_ref, pad_ref = refs
    else:
        x_ref, modx_ref, w_ref, s_ref, ox_ref, pad_ref = refs
    g = pl.program_id(1)
    wg = w_ref[0].astype(BF16)
    scale = s_ref[...]

    def segment(src_ref, mod_ref, dst_ref, window):
        length = src_ref.shape[1]
        lo = window // 2
        hi = window - 1 - lo
        h = src_ref[0] * (1.0 + mod_ref[0, 1:2, :]) + mod_ref[0, 0:1, :]
        zeros = jnp.zeros((halo, h.shape[1]), F32)
        pad_ref[0:halo, :] = zeros
        pad_ref[halo:halo + length, :] = h
        pad_ref[halo + length:2 * halo + length, :] = zeros
        tot = h
        for off in range(-lo, hi + 1):
            if off != 0:
                tot = tot + pad_ref[halo + off:halo + off + length, :]
        t = lax.broadcasted_iota(jnp.int32, h.shape, 0)
        cnt = jnp.minimum(t + hi + 1, length) - jnp.maximum(t - lo, 0)
        p = tot / cnt.astype(F32) - h
        dst_ref[0] = (_dot(p.astype(BF16), wg) * scale).astype(dst_ref.dtype)

    for gi, window in enumerate(POOL_WINDOWS):
        @pl.when(g == gi)
        def _(window=window):
            if with_ctx:
                segment(c_ref, modc_ref, oc_ref, window)
            segment(x_ref, modx_ref, ox_ref, window)


def _pool_mixer(ctx, x, mod_all, layer, w_groups, scales, j):
    b, seq, d = x.shape
    _, ng, gd, _ = w_groups.shape
    halo = 8
    with_ctx = ctx is not None
    n_ctx = ctx.shape[1] if with_ctx else 0
    mod_spec = lambda row: pl.BlockSpec((None, 1, N_MOD, gd), lambda i, g: (layer, row(i), 0, g))
    in_specs, args = [], []
    if with_ctx:
        in_specs.append(pl.BlockSpec((1, n_ctx, gd), lambda i, g: (i, 0, g)))
        args.append(ctx)
    in_specs.append(pl.BlockSpec((1, seq, gd), lambda i, g: (i, 0, g)))
    args.append(x)
    if with_ctx:
        in_specs.append(mod_spec(lambda i: b))
        args.append(mod_all)
    in_specs += [
        mod_spec(lambda i: i),
        pl.BlockSpec((None, 1, gd, gd), lambda i, g: (j, g, 0, 0)),
        pl.BlockSpec((None, 1, gd), lambda i, g: (j, 0, g)),
    ]
    args += [mod_all, w_groups, scales.reshape(scales.shape[0], 1, d)]
    out_specs = [pl.BlockSpec((1, seq, gd), lambda i, g: (i, 0, g))]
    out_shape = [jax.ShapeDtypeStruct((b, seq, d), BF16)]
    if with_ctx:
        out_specs.insert(0, pl.BlockSpec((1, n_ctx, gd), lambda i, g: (i, 0, g)))
        out_shape.insert(0, jax.ShapeDtypeStruct((b, n_ctx, d), BF16))
    kern = functools.partial(_pool_kernel, with_ctx=with_ctx, halo=halo)
    outs = pl.pallas_call(
        kern,
        grid=(b, ng),
        in_specs=in_specs,
        out_specs=out_specs,
        out_shape=out_shape,
        scratch_shapes=[pltpu.VMEM((seq + 2 * halo, gd), F32)],
        compiler_params=pltpu.CompilerParams(
            dimension_semantics=("arbitrary", "arbitrary"), vmem_limit_bytes=VMEM_LIMIT),
        name="pool_mixer",
    )(*args)
    return (outs[0], outs[1]) if with_ctx else (None, outs[0])


def _proj_kernel(c_ref, x_ref, mod_ref, w_ref, o_ref, *, n_ctx_tiles, n_chunk, first_chunk_scale):
    def run(src_ref):
        h = (src_ref[0] * (1.0 + mod_ref[0, 1:2, :]) + mod_ref[0, 0:1, :]).astype(BF16)
        n = w_ref.shape[1]
        for c0 in range(0, n, n_chunk):
            r = _dot(h, w_ref[:, c0:c0 + n_chunk])
            if c0 == 0 and first_chunk_scale != 1.0:
                r = r * first_chunk_scale
            o_ref[0, :, c0:c0 + n_chunk] = r.astype(o_ref.dtype)

    _per_segment(pl.program_id(1), n_ctx_tiles, lambda: run(c_ref), lambda: run(x_ref))


def _per_segment(j, n_ctx_tiles, ctx_fn, latent_fn):
    pl.when(j < n_ctx_tiles)(ctx_fn)
    pl.when(j >= n_ctx_tiles)(latent_fn)


def _segment_specs(b, d, n_ctx_tiles, layer):
    return [
        pl.BlockSpec((1, TOKEN_TILE, d), lambda i, j: (i, jnp.minimum(j, n_ctx_tiles - 1), 0)),
        pl.BlockSpec((1, TOKEN_TILE, d), lambda i, j: (i, jnp.maximum(j - n_ctx_tiles, 0), 0)),
        pl.BlockSpec((None, 1, N_MOD, d), lambda i, j: (layer, jnp.where(j < n_ctx_tiles, b, i), 0, 0)),
    ]


def _modulated_proj(ctx, x, mod_all, layer, w_all, j_w, out_dtype, n_chunk, first_chunk_scale=1.0):
    b, seq, d = x.shape
    n_ctx = ctx.shape[1]
    n = w_all.shape[2]
    nct = n_ctx // TOKEN_TILE
    kern = functools.partial(_proj_kernel, n_ctx_tiles=nct, n_chunk=n_chunk,
                             first_chunk_scale=first_chunk_scale)
    return pl.pallas_call(
        kern,
        grid=(b, (n_ctx + seq) // TOKEN_TILE),
        in_specs=_segment_specs(b, d, nct, layer) + [_resident((d, n), j_w)],
        out_specs=pl.BlockSpec((1, TOKEN_TILE, n), lambda i, j: (i, j, 0)),
        out_shape=jax.ShapeDtypeStruct((b, n_ctx + seq, n), out_dtype),
        compiler_params=pltpu.CompilerParams(
            dimension_semantics=("arbitrary", "arbitrary"), vmem_limit_bytes=VMEM_LIMIT),
        name="modulated_proj",
    )(ctx, x, mod_all, w_all)


def _ffn_kernel(*refs, with_ctx, n_ctx_tiles, has_mix, ffn_hidden):
    refs = list(refs)
    if with_ctx:
        c_ref, x_ref, yc_ref, yx_ref = refs[:4]
        refs = refs[4:]
    else:
        x_ref, yx_ref = refs[:2]
        refs = refs[2:]
    mod_ref, lng_ref, lnb_ref = refs[:3]
    wmix_ref = refs[3] if has_mix else None
    win_ref, wout_ref = refs[3 + has_mix:5 + has_mix]
    outs = refs[5 + has_mix:]

    def run(src_ref, y_ref, dst_ref):
        g1 = mod_ref[0, 2:3, :]
        sh2 = mod_ref[0, 3:4, :]
        sc2 = mod_ref[0, 4:5, :]
        g2 = mod_ref[0, 5:6, :]
        for r0 in range(0, src_ref.shape[1], TOKEN_TILE):
            rows = slice(r0, r0 + TOKEN_TILE)
            x = src_ref[0, rows, :]
            if has_mix:
                ox = _dot(y_ref[0, rows, :], wmix_ref[...])
            else:
                ox = y_ref[0, rows, :]
            x1 = _layer_norm(ALPHA * x + g1 * ox, lng_ref[0:1, :], lnb_ref[0:1, :])
            h2 = (x1 * (1.0 + sc2) + sh2).astype(BF16)
            gu = _dot(h2, win_ref[...])
            act = (_silu(gu[:, :ffn_hidden]) * gu[:, ffn_hidden:]).astype(BF16)
            y2 = _dot(act, wout_ref[...])
            dst_ref[0, rows, :] = _layer_norm(ALPHA * x1 + g2 * y2, lng_ref[1:2, :], lnb_ref[1:2, :])

    if with_ctx:
        _per_segment(pl.program_id(1), n_ctx_tiles,
                     lambda: run(c_ref, yc_ref, outs[0]), lambda: run(x_ref, yx_ref, outs[1]))
    else:
        run(x_ref, yx_ref, outs[0])


def _ffn_block(ctx, x, y_ctx, y_x, mod_all, layer, ln_g, ln_b, w_mix_all, j_mix, w_in_all, w_out_all):
    b, seq, d = x.shape
    f = w_out_all.shape[1]
    with_ctx = ctx is not None
    nct = ctx.shape[1] // TOKEN_TILE if with_ctx else 0
    has_mix = w_mix_all is not None
    step = FFN_STEP_TILES * TOKEN_TILE
    assert seq % step == 0
    c_map = lambda i, j: (i, jnp.minimum(j, nct - 1), 0)
    x_map = lambda i, j: (i, jnp.maximum(j - nct, 0), 0)
    c_spec = pl.BlockSpec((1, TOKEN_TILE, d), c_map)
    x_spec = pl.BlockSpec((1, step, d), x_map)
    if with_ctx:
        in_specs = [c_spec, x_spec, c_spec, x_spec,
                    pl.BlockSpec((None, 1, N_MOD, d), lambda i, j: (layer, jnp.where(j < nct, b, i), 0, 0))]
        args = [ctx, x, y_ctx, y_x, mod_all]
        out_specs = [c_spec, x_spec]
        out_shape = [jax.ShapeDtypeStruct(ctx.shape, F32), jax.ShapeDtypeStruct(x.shape, F32)]
    else:
        in_specs = [x_spec, x_spec, pl.BlockSpec((None, 1, N_MOD, d), lambda i, j: (layer, i, 0, 0))]
        args = [x, y_x, mod_all]
        out_specs = [x_spec]
        out_shape = [jax.ShapeDtypeStruct(x.shape, F32)]
    in_specs += [_resident((2, d), layer), _resident((2, d), layer)]
    args += [ln_g, ln_b]
    if has_mix:
        in_specs.append(_resident((d, d), j_mix))
        args.append(w_mix_all)
    in_specs += [_resident((d, 2 * f), layer), _resident((f, d), layer)]
    args += [w_in_all, w_out_all]
    kern = functools.partial(_ffn_kernel, with_ctx=with_ctx, n_ctx_tiles=nct, has_mix=has_mix, ffn_hidden=f)
    outs = pl.pallas_call(
        kern,
        grid=(b, nct + seq // step),
        in_specs=in_specs,
        out_specs=out_specs,
        out_shape=out_shape,
        compiler_params=pltpu.CompilerParams(
            dimension_semantics=("arbitrary", "arbitrary"), vmem_limit_bytes=VMEM_LIMIT),
        name="ffn_block",
    )(*args)
    return (outs[0], outs[1]) if with_ctx else (None, outs[0])


def _hgrn_constants():
    c = HGRN_CHUNK
    t = np.arange(c)[:, None]
    u = np.arange(c)[None, :]
    mats = [u <= t]
    masks = []
    m = c // 2
    while m >= 1:
        ref = (t // (2 * m)) * 2 * m + m - 1
        second = (t % (2 * m)) >= m
        mats.append(np.where(second, (u > ref) & (u <= t), (u > t) & (u <= ref)))
        masks.append(((t // (2 * m)) == (u // (2 * m))) & second & ((u % (2 * m)) < m))
        m //= 2
    mats.append(u > t)
    masks += [np.broadcast_to(t == u, (c, c)), np.zeros((c, c), bool)]
    pair = lambda ms: np.stack([np.concatenate([ms[2 * p], ms[2 * p + 1]], 1) for p in range(len(ms) // 2)])
    flip = lambda ms: [a[::-1, ::-1] for a in ms]
    dmat = np.stack([np.concatenate(mats, 0), np.concatenate(flip(mats), 0)])
    dmat = np.concatenate([dmat] * 3, axis=2)
    mask = np.stack([pair(masks), pair(flip(masks))])
    return dmat.astype(np.float32), mask.astype(np.float32)


def _hgrn_scan_kernel(q_ref, v_ref, zf_ref, zb_ref, g_ref, lbl_ref, nw_ref, dm_ref, mk_ref,
                      yc_ref, yx_ref, o_ref, oi_ref, qx_ref, ke_ref, r_ref, qf_ref, kf_ref, x_ref, u_ref, dec_ref,
                      *, layer, n_ctx):
    c = HGRN_CHUNK
    hd = HGRN_HEAD_DIM
    t_all = q_ref.shape[1]
    n_chunks = t_all // c
    n_ctx_chunks = n_ctx // c
    n_pairs = mk_ref.shape[1]
    n_blocks = dm_ref.shape[1] // c

    def lower_bound(direction):
        rows = [lbl_ref[direction, j:j + 1, :] for j in range(lbl_ref.shape[1])]
        mx = functools.reduce(jnp.maximum, rows)
        e = [jnp.exp(r - mx) for r in rows]
        tot = functools.reduce(lambda a, b_: a + b_, e)
        p = [ei / tot for ei in e]
        acc = p[0]
        for j in range(1, layer + 1):
            acc = acc + p[j]
        return acc - p[0]

    lb = (lower_bound(0), lower_bound(1))
    z_refs = (zf_ref, zb_ref)
    end_row = (c - 1, 0)
    zeros_k = jnp.zeros((c, hd), BF16)

    def decays(i, slot):
        rows2 = pl.ds(pl.multiple_of(i * 2 * c, 2 * c), 2 * c)
        q = _silu(q_ref[0, rows2, :])
        qf_ref[slot] = q.astype(BF16)
        for d in range(2):
            f = lb[d] + (1.0 - lb[d]) * _sigmoid(z_refs[d][0, rows2, :])
            k = 1.0 - f
            kf_ref[slot, d] = k.astype(BF16)
            g = jnp.log(f) * LOG2E
            hi = g.astype(BF16)
            r1 = g - hi.astype(F32)
            mid = r1.astype(BF16)
            lo = (r1 - mid.astype(F32)).astype(BF16)
            side = lambda a: jnp.concatenate([a[:c], a[c:]], axis=1)
            x2 = jnp.exp2(_dot(dm_ref[d], jnp.concatenate([side(hi), side(mid), side(lo)], axis=0)))
            for cc in range(2):
                xs = x2[:, cc * hd:(cc + 1) * hd]
                rows = pl.ds(pl.multiple_of(i * 2 * c + cc * c, c), c)
                x_ref[slot, d, cc] = xs[c:(n_blocks - 1) * c].astype(BF16)
                qx_ref[d, rows, :] = (q[cc * c:(cc + 1) * c] * xs[0:c]).astype(BF16)
                ke_ref[d, rows, :] = (k[cc * c:(cc + 1) * c] * xs[(n_blocks - 1) * c:]).astype(BF16)
                dec_ref[d, 2 * i + cc] = jnp.broadcast_to(xs[end_row[d]:end_row[d] + 1], (8, hd))

    def interactions(i, slot):
        for cc in range(2):
            rows = pl.ds(pl.multiple_of(i * 2 * c + cc * c, c), c)
            qc = qf_ref[slot, cc * c:(cc + 1) * c, :]
            for d in range(2):
                kc = kf_ref[slot, d, cc * c:(cc + 1) * c, :]
                xs = x_ref[slot, d, cc]
                diag = jnp.sum(qc.astype(F32) * kc.astype(F32), axis=-1, keepdims=True)
                r = mk_ref[d, n_pairs - 1] * diag
                for p in range(n_pairs - 1):
                    xa = xs[2 * p * c:(2 * p + 1) * c]
                    xb = xs[(2 * p + 1) * c:(2 * p + 2) * c]
                    lhs = jnp.concatenate([qc * xa, qc * xb], axis=1)
                    rhs = jnp.concatenate([jnp.concatenate([kc * xa, zeros_k], axis=1),
                                           jnp.concatenate([zeros_k, kc * xb], axis=1)], axis=0)
                    r = r + mk_ref[d, p] * _dot_nt(lhs, rhs)
                r_ref[d, rows, :] = r.astype(BF16)

    def values(ci, carry):
        rows = pl.ds(pl.multiple_of(ci * c, c), c)
        vc = v_ref[0, rows, :].astype(BF16)
        oo = _dot(jnp.concatenate([r_ref[0, rows, :], r_ref[1, rows, :]], axis=0),
                  jnp.concatenate([vc, vc], axis=0))
        o_ref[0, rows, :] = oo[:c]
        o_ref[1, rows, :] = oo[c:]
        uu = _dot_tn(vc, jnp.concatenate([ke_ref[0, rows, :], ke_ref[1, rows, :]], axis=1))
        u_ref[0, ci] = uu[:, :hd]
        u_ref[1, ci] = uu[:, hd:]
        return carry

    n_cpairs = n_chunks // 2

    def pipelined(k, carry):
        decays(2 * k + 1, 1)
        interactions(2 * k, 0)
        decays(2 * k + 2, 0)
        interactions(2 * k + 1, 1)
        return carry

    decays(0, 0)
    lax.fori_loop(0, n_cpairs // 2 - 1, pipelined, 0, unroll=4)
    decays(n_cpairs - 1, 1)
    interactions(n_cpairs - 2, 0)
    interactions(n_cpairs - 1, 1)
    lax.fori_loop(0, n_chunks, values, 0, unroll=True)

    def inter(i, states):
        cb = jnp.where(i < n_ctx_chunks, n_ctx_chunks - 1 - i, n_chunks - 1 - (i - n_ctx_chunks))
        new_states = []
        for d, ci in ((0, i), (1, cb)):
            rows = pl.ds(pl.multiple_of(ci * c, c), c)
            st = states[d]
            oi_ref[d, rows, :] = _dot_nt(qx_ref[d, rows, :], st.astype(BF16))
            new_states.append(st * dec_ref[d, ci][0:1] + u_ref[d, ci])
        return tuple(new_states)

    zero_state = jnp.zeros((hd, hd), F32)
    lax.fori_loop(0, n_chunks, inter, (zero_state, zero_state), unroll=True)

    nw = nw_ref[...]

    def finish(dst_ref, first_row):
        def tile(j, carry):
            rows = pl.ds(pl.multiple_of(first_row + j * TOKEN_TILE, TOKEN_TILE), TOKEN_TILE)
            o = (o_ref[0, rows, :] + oi_ref[0, rows, :]) + (o_ref[1, rows, :] + oi_ref[1, rows, :])
            o = o * lax.rsqrt(jnp.mean(o * o, axis=-1, keepdims=True) + RMS_EPS)
            gate = _silu(g_ref[0, rows, :])
            out_rows = pl.ds(pl.multiple_of(j * TOKEN_TILE, TOKEN_TILE), TOKEN_TILE)
            dst_ref[0, out_rows, :] = (o * nw * gate).astype(dst_ref.dtype)
            return carry

        lax.fori_loop(0, dst_ref.shape[1] // TOKEN_TILE, tile, 0)

    finish(yc_ref, 0)
    finish(yx_ref, n_ctx)


def _hgrn_scan(proj, lb_logits, norm_w_all, j_w, *, layer, n_ctx):
    b, t, n = proj.shape
    d = n // 5
    hd = HGRN_HEAD_DIM
    heads = d // hd
    n_chunks = t // HGRN_CHUNK
    assert n_chunks % 4 == 0
    dmat, mask = _hgrn_constants()
    part = lambda p: pl.BlockSpec((1, t, hd), lambda i, h: (i, 0, p * heads + h))
    kern = functools.partial(_hgrn_scan_kernel, layer=layer, n_ctx=n_ctx)
    return pl.pallas_call(
        kern,
        grid=(b, heads),
        in_specs=[
            part(0), part(1), part(2), part(3), part(4),
            pl.BlockSpec((2, lb_logits.shape[1], hd), lambda i, h: (0, 0, h)),
            pl.BlockSpec((None, 1, hd), lambda i, h: (j_w, 0, h)),
            _resident(dmat.shape),
            _resident(mask.shape),
        ],
        out_specs=[pl.BlockSpec((1, n_ctx, hd), lambda i, h: (i, 0, h)),
                   pl.BlockSpec((1, t - n_ctx, hd), lambda i, h: (i, 0, h))],
        out_shape=[jax.ShapeDtypeStruct((b, n_ctx, d), BF16),
                   jax.ShapeDtypeStruct((b, t - n_ctx, d), BF16)],
        scratch_shapes=[
            pltpu.VMEM((2, t, hd), F32),
            pltpu.VMEM((2, t, hd), F32),
            pltpu.VMEM((2, t, hd), BF16),
            pltpu.VMEM((2, t, hd), BF16),
            pltpu.VMEM((2, t, hd), BF16),
            pltpu.VMEM((2, 2 * HGRN_CHUNK, hd), BF16),
            pltpu.VMEM((2, 2, 2 * HGRN_CHUNK, hd), BF16),
            pltpu.VMEM((2, 2, 2, (dmat.shape[1] // HGRN_CHUNK - 2) * HGRN_CHUNK, hd), BF16),
            pltpu.VMEM((2, n_chunks, hd, hd), F32),
            pltpu.VMEM((2, n_chunks, 8, hd), F32),
        ],
        compiler_params=pltpu.CompilerParams(
            dimension_semantics=("arbitrary", "arbitrary"), vmem_limit_bytes=VMEM_LIMIT),
        name="hgrn_scan",
    )(proj, proj, proj, proj, proj, lb_logits, norm_w_all.reshape(norm_w_all.shape[0], 1, d),
      jnp.asarray(dmat, BF16), jnp.asarray(mask, F32))


def _na_bias_table(rpb):
    qcol = np.arange(GRID_W)[:, None]
    kcol = np.arange(GRID_W)[None, :]
    win = np.clip(qcol - NA_COLS // 2, 0, GRID_W - NA_COLS)
    ok = (kcol >= win) & (kcol < win + NA_COLS)
    dcol = np.arange(2 * NA_COLS - 1)[:, None, None]
    select = ((kcol - qcol + NA_COLS - 1)[None] == dcol) & ok[None]
    per_row = jnp.einsum('hrc,cqk->hrqk', rpb.astype(F32), jnp.asarray(select, F32),
                         precision=lax.Precision.HIGHEST)
    per_row = jnp.where(jnp.asarray(ok)[None, None], per_row * LOG2E, NEG_BIG)
    return jnp.concatenate([per_row[:, :-1], per_row[:, 1:]], axis=-1)


def _na_kernel(q_ref, k_ref, v_ref, bias_ref, yc_ref, yx_ref, qs_ref, s_ref, p_ref, l_ref, *, n_ctx, rows):
    w = GRID_W
    kr = min(NA_ROWS, rows)
    n_loc = kr * w
    grp = NA_ROW_GROUP
    lanes = 2 * NA_HEAD_DIM
    first = lax.broadcasted_iota(jnp.int32, (1, lanes), 1) < NA_HEAD_DIM

    def split_heads(qv):
        zero = jnp.zeros_like(qv)
        return jnp.concatenate([jnp.where(first, qv, zero), jnp.where(first, zero, qv)], axis=0)

    def merge_heads(o, n):
        return jnp.where(first, o[:n], o[n:])

    def window(r):
        rs = jnp.clip(r - kr // 2, 0, rows - kr)
        return r - rs, pl.ds(pl.multiple_of(n_ctx + rs * w, w), n_loc)

    kc = k_ref[0, 0:n_ctx, :]
    vc = v_ref[0, 0:n_ctx, :]

    s = _dot_nt(split_heads(q_ref[0, 0:n_ctx, :]), kc)
    p = jnp.exp2(s - jnp.max(s, axis=-1, keepdims=True))
    o = _dot(p.astype(BF16), vc) / jnp.sum(p, axis=-1, keepdims=True)
    yc_ref[0] = merge_heads(o, n_ctx).astype(yc_ref.dtype)

    def prep(r, carry):
        qs_ref[r] = split_heads(q_ref[0, pl.ds(pl.multiple_of(n_ctx + r * w, w), w), :])
        return carry

    def scores(g, slot):
        for rr in range(grp):
            case, keys = window(g * grp + rr)
            head_bias = lambda hh: jnp.concatenate(
                [bias_ref[hh, NA_ROWS - 1 - case + a] for a in range(0, kr, 2)], axis=1)
            bias = jnp.concatenate([head_bias(0), head_bias(1)], axis=0)
            s_ref[slot, rr, :, 0:n_loc] = _dot_nt(qs_ref[g * grp + rr], k_ref[0, keys, :]) + bias
        qg = qs_ref[pl.ds(g * grp, grp)].reshape(grp * 2 * w, lanes)
        s_ref[slot, :, :, n_loc:] = _dot_nt(qg, kc).reshape(grp, 2 * w, n_ctx)

    def softmax(slot):
        for rr in range(grp):
            sr = s_ref[slot, rr]
            tiles = [sr[:, i * lanes:(i + 1) * lanes] for i in range(sr.shape[1] // lanes)]
            m = jnp.max(functools.reduce(jnp.maximum, tiles), axis=-1, keepdims=True)
            pr = [jnp.exp2(tile - m) for tile in tiles]
            den = jnp.sum(functools.reduce(lambda u, v_: u + v_, pr), axis=-1, keepdims=True)
            l_ref[slot, rr] = jnp.broadcast_to(den, (2 * w, lanes))
            p_ref[slot, rr] = jnp.concatenate(pr, axis=1).astype(BF16)

    def values(g, slot):
        pg = p_ref[slot, :, :, n_loc:].reshape(grp * 2 * w, n_ctx)
        o_ctx = _dot(pg, vc).reshape(grp, 2 * w, lanes)
        for rr in range(grp):
            _, keys = window(g * grp + rr)
            orow = (_dot(p_ref[slot, rr, :, 0:n_loc], v_ref[0, keys, :]) + o_ctx[rr]) / l_ref[slot, rr]
            out_rows = pl.ds(pl.multiple_of((g * grp + rr) * w, w), w)
            yx_ref[0, out_rows, :] = merge_heads(orow, w).astype(yx_ref.dtype)

    n_groups = rows // grp

    def pipelined(k, carry):
        g = 2 * k
        scores(g + 2, 0)
        softmax(1)
        values(g, 0)
        scores(g + 3, 1)
        softmax(0)
        values(g + 1, 1)
        return carry

    lax.fori_loop(0, rows, prep, 0, unroll=8)
    scores(0, 0)
    scores(1, 1)
    softmax(0)
    lax.fori_loop(0, n_groups // 2 - 1, pipelined, 0, unroll=True)
    softmax(1)
    values(n_groups - 2, 0)
    values(n_groups - 1, 1)


def _na_attention(qkv, bias, *, n_ctx):
    b, t, n = qkv.shape
    d = n // 3
    lanes = 2 * NA_HEAD_DIM
    pairs = d // lanes
    rows = (t - n_ctx) // GRID_W
    assert rows % NA_ROW_GROUP == 0 and rows >= NA_ROWS and NA_ROWS % 2 == 0
    n_keys = NA_ROWS * GRID_W + n_ctx
    part = lambda p: pl.BlockSpec((1, t, lanes), lambda h, i: (i, 0, p * pairs + h))
    kern = functools.partial(_na_kernel, n_ctx=n_ctx, rows=rows)
    return pl.pallas_call(
        kern,
        grid=(pairs, b),
        in_specs=[
            part(0), part(1), part(2),
            pl.BlockSpec((2,) + bias.shape[1:], lambda h, i: (h, 0, 0, 0)),
        ],
        out_specs=[pl.BlockSpec((1, n_ctx, lanes), lambda h, i: (i, 0, h)),
                   pl.BlockSpec((1, t - n_ctx, lanes), lambda h, i: (i, 0, h))],
        out_shape=[jax.ShapeDtypeStruct((b, n_ctx, d), BF16),
                   jax.ShapeDtypeStruct((b, t - n_ctx, d), BF16)],
        scratch_shapes=[
            pltpu.VMEM((rows, 2 * GRID_W, lanes), BF16),
            pltpu.VMEM((2, NA_ROW_GROUP, 2 * GRID_W, n_keys), F32),
            pltpu.VMEM((2, NA_ROW_GROUP, 2 * GRID_W, n_keys), BF16),
            pltpu.VMEM((2, NA_ROW_GROUP, 2 * GRID_W, lanes), F32),
        ],
        compiler_params=pltpu.CompilerParams(
            dimension_semantics=("arbitrary", "arbitrary"), vmem_limit_bytes=VMEM_LIMIT),
        name="na_attention",
    )(qkv, qkv, qkv, bias)


def kernel(x, c, ctx, c_ctx, mod_w, mod_b, ln_g, ln_b, ffn_w_in, ffn_w_out, pool_w, pool_scale,
           hgrn_w_in, hgrn_lb_logits, hgrn_norm_w, hgrn_w_out, na_w_qkv, na_rpb, na_w_out):
    b, seq, d = x.shape
    n_ctx = ctx.shape[1]
    depth = mod_w.shape[0]
    assert n_ctx % TOKEN_TILE == 0 and seq % TOKEN_TILE == 0 and b < MOD_ROWS
    assert seq % GRID_W == 0 and n_ctx % HGRN_CHUNK == 0 and seq % HGRN_CHUNK == 0

    cc = jnp.concatenate([c, c_ctx[None, :], jnp.zeros((MOD_ROWS - b - 1, d), c.dtype)], axis=0)
    mod_all = _modulation(cc, mod_w, mod_b).reshape(depth, MOD_ROWS, N_MOD, d)
    w_in_all = ffn_w_in.astype(BF16)
    w_out_all = ffn_w_out.astype(BF16)
    hgrn_w_in_all = hgrn_w_in.astype(BF16)
    hgrn_w_out_all = hgrn_w_out.astype(BF16)
    na_w_qkv_all = na_w_qkv.astype(BF16)
    na_w_out_all = na_w_out.astype(BF16)

    for i in range(depth):
        last = i == depth - 1
        kind = i % N_MIXERS
        j = i // N_MIXERS
        w_mix_all = None
        if kind == 0:
            y_ctx, y_x = _pool_mixer(None if last else ctx, x, mod_all, i, pool_w, pool_scale, j)
        elif kind == 1:
            proj = _modulated_proj(ctx, x, mod_all, i, hgrn_w_in_all, j, F32, n_chunk=d)
            y_ctx, y_x = _hgrn_scan(proj, hgrn_lb_logits, hgrn_norm_w, j, layer=i, n_ctx=n_ctx)
            w_mix_all = hgrn_w_out_all
        else:
            qkv = _modulated_proj(ctx, x, mod_all, i, na_w_qkv_all, j, BF16, n_chunk=d,
                                  first_chunk_scale=NA_HEAD_DIM ** -0.5 * LOG2E)
            y_ctx, y_x = _na_attention(qkv, _na_bias_table(na_rpb[j]), n_ctx=n_ctx)
            w_mix_all = na_w_out_all
        ctx, x = _ffn_block(None if last else ctx, x, y_ctx, y_x, mod_all, i, ln_g, ln_b,
                            w_mix_all, j, w_in_all, w_out_all)
    return x
```

```python
import functools

import numpy as np
import jax
import jax.numpy as jnp
from jax import lax
from jax.experimental import pallas as pl
from jax.experimental.pallas import tpu as pltpu

DEPTH = 4
N_MIXERS = 3
N_MOD = 6
ALPHA = (2 * DEPTH) ** 0.25
LN_EPS = 1e-6
RMS_EPS = 1e-6
POOL_WINDOWS = (2, 4, 8, 16)
HGRN_HEAD_DIM = 128
HGRN_CHUNK = 64
NA_HEAD_DIM = 64
NA_ROWS = 8
NA_COLS = 16
GRID_W = 64
NEG_BIG = -1e30
LOG2E = 1.4426950408889634
NA_ROW_GROUP = 4

TOKEN_TILE = 256
FFN_STEP_TILES = 2
MOD_ROWS = 16
VMEM_LIMIT = 56 * 1024 * 1024

F32 = jnp.float32
BF16 = jnp.bfloat16


def _sigmoid(x):
    return 1.0 / (1.0 + jnp.exp(-x))


def _silu(x):
    return x * _sigmoid(x)


def _layer_norm(v, g, b):
    mu = jnp.mean(v, axis=-1, keepdims=True)
    d = v - mu
    var = jnp.mean(d * d, axis=-1, keepdims=True)
    return d * lax.rsqrt(var + LN_EPS) * g + b


def _dot(a, b):
    return jnp.dot(a, b, preferred_element_type=F32)


def _dot_nt(a, b):
    return lax.dot_general(a, b, (((1,), (1,)), ((), ())), preferred_element_type=F32)


def _dot_tn(a, b):
    return lax.dot_general(a, b, (((0,), (0,)), ((), ())), preferred_element_type=F32)


def _resident(shape, layer=None):
    zeros = (0,) * len(shape)
    if layer is None:
        return pl.BlockSpec(shape, lambda *_: zeros, pipeline_mode=pl.Buffered(1))
    return pl.BlockSpec((None,) + tuple(shape), lambda *_: (layer,) + zeros, pipeline_mode=pl.Buffered(1))


def _mod_kernel(c_ref, w_ref, b_ref, o_ref):
    s = _silu(c_ref[...]).astype(BF16)
    o_ref[0] = _dot(s, w_ref[0].astype(BF16)) + b_ref[0]


def _modulation(cc, mod_w, mod_b):
    depth, d, n = mod_w.shape
    tn = n // 4
    return pl.pallas_call(
        _mod_kernel,
        grid=(depth, n // tn),
        in_specs=[
            pl.BlockSpec((MOD_ROWS, d), lambda i, j: (0, 0)),
            pl.BlockSpec((1, d, tn), lambda i, j: (i, 0, j)),
            pl.BlockSpec((1, 1, tn), lambda i, j: (i, 0, j)),
        ],
        out_specs=pl.BlockSpec((1, MOD_ROWS, tn), lambda i, j: (i, 0, j)),
        out_shape=jax.ShapeDtypeStruct((depth, MOD_ROWS, n), F32),
        compiler_params=pltpu.CompilerParams(
            dimension_semantics=("arbitrary", "arbitrary"), vmem_limit_bytes=VMEM_LIMIT),
        name="modulation",
    )(cc, mod_w, mod_b.reshape(depth, 1, n))


def _pool_kernel(*refs, with_ctx, halo):
    if with_ctx:
        c_ref, x_ref, modc_ref, modx_ref, w_ref, s_ref, oc_ref, ox_ref, pad_ref = refs
    else:
        x_ref, modx_ref, w_ref, s_ref, ox_ref, pad_ref = refs
    g = pl.program_id(1)
    wg = w_ref[0].astype(BF16)
    scale = s_ref[...]

    def segment(src_ref, mod_ref, dst_ref, window):
        length = src_ref.shape[1]
        lo = window // 2
        hi = window - 1 - lo
        h = src_ref[0] * (1.0 + mod_ref[0, 1:2, :]) + mod_ref[0, 0:1, :]
        zeros = jnp.zeros((halo, h.shape[1]), F32)
        pad_ref[0:halo, :] = zeros
        pad_ref[halo:halo + length, :] = h
        pad_ref[halo + length:2 * halo + length, :] = zeros
        tot = h
        for off in range(-lo, hi + 1):
            if off != 0:
                tot = tot + pad_ref[halo + off:halo + off + length, :]
        t = lax.broadcasted_iota(jnp.int32, h.shape, 0)
        cnt = jnp.minimum(t + hi + 1, length) - jnp.maximum(t - lo, 0)
        p = tot / cnt.astype(F32) - h
        dst_ref[0] = (_dot(p.astype(BF16), wg) * scale).astype(dst_ref.dtype)

    for gi, window in enumerate(POOL_WINDOWS):
        @pl.when(g == gi)
        def _(window=window):
            if with_ctx:
                segment(c_ref, modc_ref, oc_ref, window)
            segment(x_ref, modx_ref, ox_ref, window)


def _pool_mixer(ctx, x, mod_all, layer, w_groups, scales, j):
    b, seq, d = x.shape
    _, ng, gd, _ = w_groups.shape
    halo = 8
    with_ctx = ctx is not None
    n_ctx = ctx.shape[1] if with_ctx else 0
    mod_spec = lambda row: pl.BlockSpec((None, 1, N_MOD, gd), lambda i, g: (layer, row(i), 0, g))
    in_specs, args = [], []
    if with_ctx:
        in_specs.append(pl.BlockSpec((1, n_ctx, gd), lambda i, g: (i, 0, g)))
        args.append(ctx)
    in_specs.append(pl.BlockSpec((1, seq, gd), lambda i, g: (i, 0, g)))
    args.append(x)
    if with_ctx:
        in_specs.append(mod_spec(lambda i: b))
        args.append(mod_all)
    in_specs += [
        mod_spec(lambda i: i),
        pl.BlockSpec((None, 1, gd, gd), lambda i, g: (j, g, 0, 0)),
        pl.BlockSpec((None, 1, gd), lambda i, g: (j, 0, g)),
    ]
    args += [mod_all, w_groups, scales.reshape(scales.shape[0], 1, d)]
    out_specs = [pl.BlockSpec((1, seq, gd), lambda i, g: (i, 0, g))]
    out_shape = [jax.ShapeDtypeStruct((b, seq, d), BF16)]
    if with_ctx:
        out_specs.insert(0, pl.BlockSpec((1, n_ctx, gd), lambda i, g: (i, 0, g)))
        out_shape.insert(0, jax.ShapeDtypeStruct((b, n_ctx, d), BF16))
    kern = functools.partial(_pool_kernel, with_ctx=with_ctx, halo=halo)
    outs = pl.pallas_call(
        kern,
        grid=(b, ng),
        in_specs=in_specs,
        out_specs=out_specs,
        out_shape=out_shape,
        scratch_shapes=[pltpu.VMEM((seq + 2 * halo, gd), F32)],
        compiler_params=pltpu.CompilerParams(
            dimension_semantics=("arbitrary", "arbitrary"), vmem_limit_bytes=VMEM_LIMIT),
        name="pool_mixer",
    )(*args)
    return (outs[0], outs[1]) if with_ctx else (None, outs[0])


def _proj_kernel(c_ref, x_ref, mod_ref, w_ref, o_ref, *, n_ctx_tiles, n_chunk, first_chunk_scale):
    def run(src_ref):
        h = (src_ref[0] * (1.0 + mod_ref[0, 1:2, :]) + mod_ref[0, 0:1, :]).astype(BF16)
        n = w_ref.shape[1]
        for c0 in range(0, n, n_chunk):
            r = _dot(h, w_ref[:, c0:c0 + n_chunk])
            if c0 == 0 and first_chunk_scale != 1.0:
                r = r * first_chunk_scale
            o_ref[0, :, c0:c0 + n_chunk] = r.astype(o_ref.dtype)

    _per_segment(pl.program_id(1), n_ctx_tiles, lambda: run(c_ref), lambda: run(x_ref))


def _per_segment(j, n_ctx_tiles, ctx_fn, latent_fn):
    pl.when(j < n_ctx_tiles)(ctx_fn)
    pl.when(j >= n_ctx_tiles)(latent_fn)


def _segment_specs(b, d, n_ctx_tiles, layer):
    return [
        pl.BlockSpec((1, TOKEN_TILE, d), lambda i, j: (i, jnp.minimum(j, n_ctx_tiles - 1), 0)),
        pl.BlockSpec((1, TOKEN_TILE, d), lambda i, j: (i, jnp.maximum(j - n_ctx_tiles, 0), 0)),
        pl.BlockSpec((None, 1, N_MOD, d), lambda i, j: (layer, jnp.where(j < n_ctx_tiles, b, i), 0, 0)),
    ]


def _modulated_proj(ctx, x, mod_all, layer, w_all, j_w, out_dtype, n_chunk, first_chunk_scale=1.0):
    b, seq, d = x.shape
    n_ctx = ctx.shape[1]
    n = w_all.shape[2]
    nct = n_ctx // TOKEN_TILE
    kern = functools.partial(_proj_kernel, n_ctx_tiles=nct, n_chunk=n_chunk,
                             first_chunk_scale=first_chunk_scale)
    return pl.pallas_call(
        kern,
        grid=(b, (n_ctx + seq) // TOKEN_TILE),
        in_specs=_segment_specs(b, d, nct, layer) + [_resident((d, n), j_w)],
        out_specs=pl.BlockSpec((1, TOKEN_TILE, n), lambda i, j: (i, j, 0)),
        out_shape=jax.ShapeDtypeStruct((b, n_ctx + seq, n), out_dtype),
        compiler_params=pltpu.CompilerParams(
            dimension_semantics=("arbitrary", "arbitrary"), vmem_limit_bytes=VMEM_LIMIT),
        name="modulated_proj",
    )(ctx, x, mod_all, w_all)


def _ffn_kernel(*refs, with_ctx, n_ctx_tiles, has_mix, ffn_hidden):
    refs = list(refs)
    if with_ctx:
        c_ref, x_ref, yc_ref, yx_ref = refs[:4]
        refs = refs[4:]
    else:
        x_ref, yx_ref = refs[:2]
        refs = refs[2:]
    mod_ref, lng_ref, lnb_ref = refs[:3]
    wmix_ref = refs[3] if has_mix else None
    win_ref, wout_ref = refs[3 + has_mix:5 + has_mix]
    outs = refs[5 + has_mix:]

    def run(src_ref, y_ref, dst_ref):
        g1 = mod_ref[0, 2:3, :]
        sh2 = mod_ref[0, 3:4, :]
        sc2 = mod_ref[0, 4:5, :]
        g2 = mod_ref[0, 5:6, :]
        for r0 in range(0, src_ref.shape[1], TOKEN_TILE):
            rows = slice(r0, r0 + TOKEN_TILE)
            x = src_ref[0, rows, :]
            if has_mix:
                ox = _dot(y_ref[0, rows, :], wmix_ref[...])
            else:
                ox = y_ref[0, rows, :]
            x1 = _layer_norm(ALPHA * x + g1 * ox, lng_ref[0:1, :], lnb_ref[0:1, :])
            h2 = (x1 * (1.0 + sc2) + sh2).astype(BF16)
            gu = _dot(h2, win_ref[...])
            act = (_silu(gu[:, :ffn_hidden]) * gu[:, ffn_hidden:]).astype(BF16)
            y2 = _dot(act, wout_ref[...])
            dst_ref[0, rows, :] = _layer_norm(ALPHA * x1 + g2 * y2, lng_ref[1:2, :], lnb_ref[1:2, :])

    if with_ctx:
        _per_segment(pl.program_id(1), n_ctx_tiles,
                     lambda: run(c_ref, yc_ref, outs[0]), lambda: run(x_ref, yx_ref, outs[1]))
    else:
        run(x_ref, yx_ref, outs[0])


def _ffn_block(ctx, x, y_ctx, y_x, mod_all, layer, ln_g, ln_b, w_mix_all, j_mix, w_in_all, w_out_all):
    b, seq, d = x.shape
    f = w_out_all.shape[1]
    with_ctx = ctx is not None
    nct = ctx.shape[1] // TOKEN_TILE if with_ctx else 0
    has_mix = w_mix_all is not None
    step = FFN_STEP_TILES * TOKEN_TILE
    assert seq % step == 0
    c_map = lambda i, j: (i, jnp.minimum(j, nct - 1), 0)
    x_map = lambda i, j: (i, jnp.maximum(j - nct, 0), 0)
    c_spec = pl.BlockSpec((1, TOKEN_TILE, d), c_map)
    x_spec = pl.BlockSpec((1, step, d), x_map)
    if with_ctx:
        in_specs = [c_spec, x_spec, c_spec, x_spec,
                    pl.BlockSpec((None, 1, N_MOD, d), lambda i, j: (layer, jnp.where(j < nct, b, i), 0, 0))]
        args = [ctx, x, y_ctx, y_x, mod_all]
        out_specs = [c_spec, x_spec]
        out_shape = [jax.ShapeDtypeStruct(ctx.shape, F32), jax.ShapeDtypeStruct(x.shape, F32)]
    else:
        in_specs = [x_spec, x_spec, pl.BlockSpec((None, 1, N_MOD, d), lambda i, j: (layer, i, 0, 0))]
        args = [x, y_x, mod_all]
        out_specs = [x_spec]
        out_shape = [jax.ShapeDtypeStruct(x.shape, F32)]
    in_specs += [_resident((2, d), layer), _resident((2, d), layer)]
    args += [ln_g, ln_b]
    if has_mix:
        in_specs.append(_resident((d, d), j_mix))
        args.append(w_mix_all)
    in_specs += [_resident((d, 2 * f), layer), _resident((f, d), layer)]
    args += [w_in_all, w_out_all]
    kern = functools.partial(_ffn_kernel, with_ctx=with_ctx, n_ctx_tiles=nct, has_mix=has_mix, ffn_hidden=f)
    outs = pl.pallas_call(
        kern,
        grid=(b, nct + seq // step),
        in_specs=in_specs,
        out_specs=out_specs,
        out_shape=out_shape,
        compiler_params=pltpu.CompilerParams(
            dimension_semantics=("arbitrary", "arbitrary"), vmem_limit_bytes=VMEM_LIMIT),
        name="ffn_block",
    )(*args)
    return (outs[0], outs[1]) if with_ctx else (None, outs[0])


def _hgrn_constants():
    c = HGRN_CHUNK
    t = np.arange(c)[:, None]
    u = np.arange(c)[None, :]
    mats = [u <= t]
    masks = []
    m = c // 2
    while m >= 1:
        ref = (t // (2 * m)) * 2 * m + m - 1
        second = (t % (2 * m)) >= m
        mats.append(np.where(second, (u > ref) & (u <= t), (u > t) & (u <= ref)))
        masks.append(((t // (2 * m)) == (u // (2 * m))) & second & ((u % (2 * m)) < m))
        m //= 2
    mats.append(u > t)
    masks += [np.broadcast_to(t == u, (c, c)), np.zeros((c, c), bool)]
    pair = lambda ms: np.stack([np.concatenate([ms[2 * p], ms[2 * p + 1]], 1) for p in range(len(ms) // 2)])
    flip = lambda ms: [a[::-1, ::-1] for a in ms]
    dmat = np.stack([np.concatenate(mats, 0), np.concatenate(flip(mats), 0)])
    dmat = np.concatenate([dmat] * 3, axis=2)
    mask = np.stack([pair(masks), pair(flip(masks))])
    return dmat.astype(np.float32), mask.astype(np.float32)


def _hgrn_scan_kernel(q_ref, v_ref, zf_ref, zb_ref, g_ref, lbl_ref, nw_ref, dm_ref, mk_ref,
                      yc_ref, yx_ref, o_ref, oi_ref, qx_ref, ke_ref, r_ref, qf_ref, kf_ref, x_ref, u_ref, dec_ref,
                      *, layer, n_ctx):
    c = HGRN_CHUNK
    hd = HGRN_HEAD_DIM
    t_all = q_ref.shape[1]
    n_chunks = t_all // c
    n_ctx_chunks = n_ctx // c
    n_pairs = mk_ref.shape[1]
    n_blocks = dm_ref.shape[1] // c

    def lower_bound(direction):
        rows = [lbl_ref[direction, j:j + 1, :] for j in range(lbl_ref.shape[1])]
        mx = functools.reduce(jnp.maximum, rows)
        e = [jnp.exp(r - mx) for r in rows]
        tot = functools.reduce(lambda a, b_: a + b_, e)
        p = [ei / tot for ei in e]
        acc = p[0]
        for j in range(1, layer + 1):
            acc = acc + p[j]
        return acc - p[0]

    lb = (lower_bound(0), lower_bound(1))
    z_refs = (zf_ref, zb_ref)
    end_row = (c - 1, 0)
    zeros_k = jnp.zeros((c, hd), BF16)

    def decays(i, slot):
        rows2 = pl.ds(pl.multiple_of(i * 2 * c, 2 * c), 2 * c)
        q = _silu(q_ref[0, rows2, :])
        qf_ref[slot] = q.astype(BF16)
        for d in range(2):
            f = lb[d] + (1.0 - lb[d]) * _sigmoid(z_refs[d][0, rows2, :])
            k = 1.0 - f
            kf_ref[slot, d] = k.astype(BF16)
            g = jnp.log(f) * LOG2E
            hi = g.astype(BF16)
            r1 = g - hi.astype(F32)
            mid = r1.astype(BF16)
            lo = (r1 - mid.astype(F32)).astype(BF16)
            side = lambda a: jnp.concatenate([a[:c], a[c:]], axis=1)
            x2 = jnp.exp2(_dot(dm_ref[d], jnp.concatenate([side(hi), side(mid), side(lo)], axis=0)))
            for cc in range(2):
                xs = x2[:, cc * hd:(cc + 1) * hd]
                rows = pl.ds(pl.multiple_of(i * 2 * c + cc * c, c), c)
                x_ref[slot, d, cc] = xs[c:(n_blocks - 1) * c].astype(BF16)
                qx_ref[d, rows, :] = (q[cc * c:(cc + 1) * c] * xs[0:c]).astype(BF16)
                ke_ref[d, rows, :] = (k[cc * c:(cc + 1) * c] * xs[(n_blocks - 1) * c:]).astype(BF16)
                dec_ref[d, 2 * i + cc] = jnp.broadcast_to(xs[end_row[d]:end_row[d] + 1], (8, hd))

    def interactions(i, slot):
        for cc in range(2):
            rows = pl.ds(pl.multiple_of(i * 2 * c + cc * c, c), c)
            qc = qf_ref[slot, cc * c:(cc + 1) * c, :]
            for d in range(2):
                kc = kf_ref[slot, d, cc * c:(cc + 1) * c, :]
                xs = x_ref[slot, d, cc]
                diag = jnp.sum(qc.astype(F32) * kc.astype(F32), axis=-1, keepdims=True)
                r = mk_ref[d, n_pairs - 1] * diag
                for p in range(n_pairs - 1):
                    xa = xs[2 * p * c:(2 * p + 1) * c]
                    xb = xs[(2 * p + 1) * c:(2 * p + 2) * c]
                    lhs = jnp.concatenate([qc * xa, qc * xb], axis=1)
                    rhs = jnp.concatenate([jnp.concatenate([kc * xa, zeros_k], axis=1),
                                           jnp.concatenate([zeros_k, kc * xb], axis=1)], axis=0)
                    r = r + mk_ref[d, p] * _dot_nt(lhs, rhs)
                r_ref[d, rows, :] = r.astype(BF16)

    def values(ci, carry):
        rows = pl.ds(pl.multiple_of(ci * c, c), c)
        vc = v_ref[0, rows, :].astype(BF16)
        oo = _dot(jnp.concatenate([r_ref[0, rows, :], r_ref[1, rows, :]], axis=0),
                  jnp.concatenate([vc, vc], axis=0))
        o_ref[0, rows, :] = oo[:c]
        o_ref[1, rows, :] = oo[c:]
        uu = _dot_tn(vc, jnp.concatenate([ke_ref[0, rows, :], ke_ref[1, rows, :]], axis=1))
        u_ref[0, ci] = uu[:, :hd]
        u_ref[1, ci] = uu[:, hd:]
        return carry

    n_cpairs = n_chunks // 2

    def pipelined(k, carry):
        decays(2 * k + 1, 1)
        interactions(2 * k, 0)
        decays(2 * k + 2, 0)
        interactions(2 * k + 1, 1)
        return carry

    decays(0, 0)
    lax.fori_loop(0, n_cpairs // 2 - 1, pipelined, 0, unroll=True)
    decays(n_cpairs - 1, 1)
    interactions(n_cpairs - 2, 0)
    interactions(n_cpairs - 1, 1)
    lax.fori_loop(0, n_chunks, values, 0, unroll=True)

    def inter(i, states):
        cb = jnp.where(i < n_ctx_chunks, n_ctx_chunks - 1 - i, n_chunks - 1 - (i - n_ctx_chunks))
        new_states = []
        for d, ci in ((0, i), (1, cb)):
            rows = pl.ds(pl.multiple_of(ci * c, c), c)
            st = states[d]
            oi_ref[d, rows, :] = _dot_nt(qx_ref[d, rows, :], st.astype(BF16))
            new_states.append(st * dec_ref[d, ci][0:1] + u_ref[d, ci])
        return tuple(new_states)

    zero_state = jnp.zeros((hd, hd), F32)
    lax.fori_loop(0, n_chunks, inter, (zero_state, zero_state), unroll=True)

    nw = nw_ref[...]

    def finish(dst_ref, first_row):
        def tile(j, carry):
            rows = pl.ds(pl.multiple_of(first_row + j * TOKEN_TILE, TOKEN_TILE), TOKEN_TILE)
            o = (o_ref[0, rows, :] + oi_ref[0, rows, :]) + (o_ref[1, rows, :] + oi_ref[1, rows, :])
            o = o * lax.rsqrt(jnp.mean(o * o, axis=-1, keepdims=True) + RMS_EPS)
            gate = _silu(g_ref[0, rows, :])
            out_rows = pl.ds(pl.multiple_of(j * TOKEN_TILE, TOKEN_TILE), TOKEN_TILE)
            dst_ref[0, out_rows, :] = (o * nw * gate).astype(dst_ref.dtype)
            return carry

        lax.fori_loop(0, dst_ref.shape[1] // TOKEN_TILE, tile, 0, unroll=4)

    finish(yc_ref, 0)
    finish(yx_ref, n_ctx)


def _hgrn_scan(proj, lb_logits, norm_w_all, j_w, *, layer, n_ctx):
    b, t, n = proj.shape
    d = n // 5
    hd = HGRN_HEAD_DIM
    heads = d // hd
    n_chunks = t // HGRN_CHUNK
    assert n_chunks % 4 == 0
    dmat, mask = _hgrn_constants()
    part = lambda p: pl.BlockSpec((1, t, hd), lambda i, h: (i, 0, p * heads + h))
    kern = functools.partial(_hgrn_scan_kernel, layer=layer, n_ctx=n_ctx)
    return pl.pallas_call(
        kern,
        grid=(b, heads),
        in_specs=[
            part(0), part(1), part(2), part(3), part(4),
            pl.BlockSpec((2, lb_logits.shape[1], hd), lambda i, h: (0, 0, h)),
            pl.BlockSpec((None, 1, hd), lambda i, h: (j_w, 0, h)),
            _resident(dmat.shape),
            _resident(mask.shape),
        ],
        out_specs=[pl.BlockSpec((1, n_ctx, hd), lambda i, h: (i, 0, h)),
                   pl.BlockSpec((1, t - n_ctx, hd), lambda i, h: (i, 0, h))],
        out_shape=[jax.ShapeDtypeStruct((b, n_ctx, d), BF16),
                   jax.ShapeDtypeStruct((b, t - n_ctx, d), BF16)],
        scratch_shapes=[
            pltpu.VMEM((2, t, hd), F32),
            pltpu.VMEM((2, t, hd), F32),
            pltpu.VMEM((2, t, hd), BF16),
            pltpu.VMEM((2, t, hd), BF16),
            pltpu.VMEM((2, t, hd), BF16),
            pltpu.VMEM((2, 2 * HGRN_CHUNK, hd), BF16),
            pltpu.VMEM((2, 2, 2 * HGRN_CHUNK, hd), BF16),
            pltpu.VMEM((2, 2, 2, (dmat.shape[1] // HGRN_CHUNK - 2) * HGRN_CHUNK, hd), BF16),
            pltpu.VMEM((2, n_chunks, hd, hd), F32),
            pltpu.VMEM((2, n_chunks, 8, hd), F32),
        ],
        compiler_params=pltpu.CompilerParams(
            dimension_semantics=("arbitrary", "arbitrary"), vmem_limit_bytes=VMEM_LIMIT),
        name="hgrn_scan",
    )(proj, proj, proj, proj, proj, lb_logits, norm_w_all.reshape(norm_w_all.shape[0], 1, d),
      jnp.asarray(dmat, BF16), jnp.asarray(mask, F32))


def _na_bias_table(rpb):
    qcol = np.arange(GRID_W)[:, None]
    kcol = np.arange(GRID_W)[None, :]
    win = np.clip(qcol - NA_COLS // 2, 0, GRID_W - NA_COLS)
    ok = (kcol >= win) & (kcol < win + NA_COLS)
    dcol = np.arange(2 * NA_COLS - 1)[:, None, None]
    select = ((kcol - qcol + NA_COLS - 1)[None] == dcol) & ok[None]
    per_row = jnp.einsum('hrc,cqk->hrqk', rpb.astype(F32), jnp.asarray(select, F32),
                         precision=lax.Precision.HIGHEST)
    per_row = jnp.where(jnp.asarray(ok)[None, None], per_row * LOG2E, NEG_BIG)
    return jnp.concatenate([per_row[:, :-1], per_row[:, 1:]], axis=-1)


def _na_kernel(q_ref, k_ref, v_ref, bias_ref, yc_ref, yx_ref, qs_ref, s_ref, p_ref, l_ref, *, n_ctx, rows):
    w = GRID_W
    kr = min(NA_ROWS, rows)
    n_loc = kr * w
    grp = NA_ROW_GROUP
    lanes = 2 * NA_HEAD_DIM
    first = lax.broadcasted_iota(jnp.int32, (1, lanes), 1) < NA_HEAD_DIM

    def split_heads(qv):
        zero = jnp.zeros_like(qv)
        return jnp.concatenate([jnp.where(first, qv, zero), jnp.where(first, zero, qv)], axis=0)

    def merge_heads(o, n):
        return jnp.where(first, o[:n], o[n:])

    def window(r):
        rs = jnp.clip(r - kr // 2, 0, rows - kr)
        return r - rs, pl.ds(pl.multiple_of(n_ctx + rs * w, w), n_loc)

    kc = k_ref[0, 0:n_ctx, :]
    vc = v_ref[0, 0:n_ctx, :]

    s = _dot_nt(split_heads(q_ref[0, 0:n_ctx, :]), kc)
    p = jnp.exp2(s - jnp.max(s, axis=-1, keepdims=True))
    o = _dot(p.astype(BF16), vc) / jnp.sum(p, axis=-1, keepdims=True)
    yc_ref[0] = merge_heads(o, n_ctx).astype(yc_ref.dtype)

    def prep(r, carry):
        qs_ref[r] = split_heads(q_ref[0, pl.ds(pl.multiple_of(n_ctx + r * w, w), w), :])
        return carry

    def scores(g, slot):
        for rr in range(grp):
            case, keys = window(g * grp + rr)
            head_bias = lambda hh: jnp.concatenate(
                [bias_ref[hh, NA_ROWS - 1 - case + a] for a in range(0, kr, 2)], axis=1)
            bias = jnp.concatenate([head_bias(0), head_bias(1)], axis=0)
            s_ref[slot, rr, :, 0:n_loc] = _dot_nt(qs_ref[g * grp + rr], k_ref[0, keys, :]) + bias
        qg = qs_ref[pl.ds(g * grp, grp)].reshape(grp * 2 * w, lanes)
        s_ref[slot, :, :, n_loc:] = _dot_nt(qg, kc).reshape(grp, 2 * w, n_ctx)

    def softmax(slot):
        for rr in range(grp):
            sr = s_ref[slot, rr]
            tiles = [sr[:, i * lanes:(i + 1) * lanes] for i in range(sr.shape[1] // lanes)]
            m = jnp.max(functools.reduce(jnp.maximum, tiles), axis=-1, keepdims=True)
            pr = [jnp.exp2(tile - m) for tile in tiles]
            den = jnp.sum(functools.reduce(lambda u, v_: u + v_, pr), axis=-1, keepdims=True)
            l_ref[slot, rr] = jnp.broadcast_to(den, (2 * w, lanes))
            p_ref[slot, rr] = jnp.concatenate(pr, axis=1).astype(BF16)

    def values(g, slot):
        pg = p_ref[slot, :, :, n_loc:].reshape(grp * 2 * w, n_ctx)
        o_ctx = _dot(pg, vc).reshape(grp, 2 * w, lanes)
        for rr in range(grp):
            _, keys = window(g * grp + rr)
            orow = (_dot(p_ref[slot, rr, :, 0:n_loc], v_ref[0, keys, :]) + o_ctx[rr]) / l_ref[slot, rr]
            out_rows = pl.ds(pl.multiple_of((g * grp + rr) * w, w), w)
            yx_ref[0, out_rows, :] = merge_heads(orow, w).astype(yx_ref.dtype)

    n_groups = rows // grp

    def pipelined(k, carry):
        g = 2 * k
        scores(g + 2, 0)
        softmax(1)
        values(g, 0)
        scores(g + 3, 1)
        softmax(0)
        values(g + 1, 1)
        return carry

    lax.fori_loop(0, rows, prep, 0, unroll=8)
    scores(0, 0)
    scores(1, 1)
    softmax(0)
    lax.fori_loop(0, n_groups // 2 - 1, pipelined, 0, unroll=True)
    softmax(1)
    values(n_groups - 2, 0)
    values(n_groups - 1, 1)


def _na_attention(qkv, bias, *, n_ctx):
    b, t, n = qkv.shape
    d = n // 3
    lanes = 2 * NA_HEAD_DIM
    pairs = d // lanes
    rows = (t - n_ctx) // GRID_W
    assert rows % NA_ROW_GROUP == 0 and rows >= NA_ROWS and NA_ROWS % 2 == 0
    n_keys = NA_ROWS * GRID_W + n_ctx
    part = lambda p: pl.BlockSpec((1, t, lanes), lambda h, i: (i, 0, p * pairs + h))
    kern = functools.partial(_na_kernel, n_ctx=n_ctx, rows=rows)
    return pl.pallas_call(
        kern,
        grid=(pairs, b),
        in_specs=[
            part(0), part(1), part(2),
            pl.BlockSpec((2,) + bias.shape[1:], lambda h, i: (h, 0, 0, 0)),
        ],
        out_specs=[pl.BlockSpec((1, n_ctx, lanes), lambda h, i: (i, 0, h)),
                   pl.BlockSpec((1, t - n_ctx, lanes), lambda h, i: (i, 0, h))],
        out_shape=[jax.ShapeDtypeStruct((b, n_ctx, d), BF16),
                   jax.ShapeDtypeStruct((b, t - n_ctx, d), BF16)],
        scratch_shapes=[
            pltpu.VMEM((rows, 2 * GRID_W, lanes), BF16),
            pltpu.VMEM((2, NA_ROW_GROUP, 2 * GRID_W, n_keys), F32),
            pltpu.VMEM((2, NA_ROW_GROUP, 2 * GRID_W, n_keys), BF16),
            pltpu.VMEM((2, NA_ROW_GROUP, 2 * GRID_W, lanes), F32),
        ],
        compiler_params=pltpu.CompilerParams(
            dimension_semantics=("arbitrary", "arbitrary"), vmem_limit_bytes=VMEM_LIMIT),
        name="na_attention",
    )(qkv, qkv, qkv, bias)


def kernel(x, c, ctx, c_ctx, mod_w, mod_b, ln_g, ln_b, ffn_w_in, ffn_w_out, pool_w, pool_scale,
           hgrn_w_in, hgrn_lb_logits, hgrn_norm_w, hgrn_w_out, na_w_qkv, na_rpb, na_w_out):
    b, seq, d = x.shape
    n_ctx = ctx.shape[1]
    depth = mod_w.shape[0]
    assert n_ctx % TOKEN_TILE == 0 and seq % TOKEN_TILE == 0 and b < MOD_ROWS
    assert seq % GRID_W == 0 and n_ctx % HGRN_CHUNK == 0 and seq % HGRN_CHUNK == 0

    cc = jnp.concatenate([c, c_ctx[None, :], jnp.zeros((MOD_ROWS - b - 1, d), c.dtype)], axis=0)
    mod_all = _modulation(cc, mod_w, mod_b).reshape(depth, MOD_ROWS, N_MOD, d)
    w_in_all = ffn_w_in.astype(BF16)
    w_out_all = ffn_w_out.astype(BF16)
    hgrn_w_in_all = hgrn_w_in.astype(BF16)
    hgrn_w_out_all = hgrn_w_out.astype(BF16)
    na_w_qkv_all = na_w_qkv.astype(BF16)
    na_w_out_all = na_w_out.astype(BF16)

    for i in range(depth):
        last = i == depth - 1
        kind = i % N_MIXERS
        j = i // N_MIXERS
        w_mix_all = None
        if kind == 0:
            y_ctx, y_x = _pool_mixer(None if last else ctx, x, mod_all, i, pool_w, pool_scale, j)
        elif kind == 1:
            proj = _modulated_proj(ctx, x, mod_all, i, hgrn_w_in_all, j, F32, n_chunk=d)
            y_ctx, y_x = _hgrn_scan(proj, hgrn_lb_logits, hgrn_norm_w, j, layer=i, n_ctx=n_ctx)
            w_mix_all = hgrn_w_out_all
        else:
            qkv = _modulated_proj(ctx, x, mod_all, i, na_w_qkv_all, j, BF16, n_chunk=d,
                                  first_chunk_scale=NA_HEAD_DIM ** -0.5 * LOG2E)
            y_ctx, y_x = _na_attention(qkv, _na_bias_table(na_rpb[j]), n_ctx=n_ctx)
            w_mix_all = na_w_out_all
        ctx, x = _ffn_block(None if last else ctx, x, y_ctx, y_x, mod_all, i, ln_g, ln_b,
                            w_mix_all, j, w_in_all, w_out_all)
    return x
```

```python
import functools

import numpy as np
import jax
import jax.numpy as jnp
from jax import lax
from jax.experimental import pallas as pl
from jax.experimental.pallas import tpu as pltpu

DEPTH = 4
N_MIXERS = 3
N_MOD = 6
ALPHA = (2 * DEPTH) ** 0.25
LN_EPS = 1e-6
RMS_EPS = 1e-6
POOL_WINDOWS = (2, 4, 8, 16)
HGRN_HEAD_DIM = 128
HGRN_CHUNK = 64
NA_HEAD_DIM = 64
NA_ROWS = 8
NA_COLS = 16
GRID_W = 64
NEG_BIG = -1e30
LOG2E = 1.4426950408889634
NA_ROW_GROUP = 4

TOKEN_TILE = 256
FFN_STEP_TILES = 4
MOD_ROWS = 16
VMEM_LIMIT = 56 * 1024 * 1024

F32 = jnp.float32
BF16 = jnp.bfloat16


def _sigmoid(x):
    return 1.0 / (1.0 + jnp.exp(-x))


def _silu(x):
    return x * _sigmoid(x)


def _layer_norm(v, g, b):
    mu = jnp.mean(v, axis=-1, keepdims=True)
    d = v - mu
    var = jnp.mean(d * d, axis=-1, keepdims=True)
    return d * lax.rsqrt(var + LN_EPS) * g + b


def _dot(a, b):
    return jnp.dot(a, b, preferred_element_type=F32)


def _dot_nt(a, b):
    return lax.dot_general(a, b, (((1,), (1,)), ((), ())), preferred_element_type=F32)


def _dot_tn(a, b):
    return lax.dot_general(a, b, (((0,), (0,)), ((), ())), preferred_element_type=F32)


def _resident(shape, layer=None):
    zeros = (0,) * len(shape)
    if layer is None:
        return pl.BlockSpec(shape, lambda *_: zeros, pipeline_mode=pl.Buffered(1))
    return pl.BlockSpec((None,) + tuple(shape), lambda *_: (layer,) + zeros, pipeline_mode=pl.Buffered(1))


def _mod_kernel(c_ref, w_ref, b_ref, o_ref):
    s = _silu(c_ref[...]).astype(BF16)
    o_ref[0] = _dot(s, w_ref[0].astype(BF16)) + b_ref[0]


def _modulation(cc, mod_w, mod_b):
    depth, d, n = mod_w.shape
    tn = n // 4
    return pl.pallas_call(
        _mod_kernel,
        grid=(depth, n // tn),
        in_specs=[
            pl.BlockSpec((MOD_ROWS, d), lambda i, j: (0, 0)),
            pl.BlockSpec((1, d, tn), lambda i, j: (i, 0, j)),
            pl.BlockSpec((1, 1, tn), lambda i, j: (i, 0, j)),
        ],
        out_specs=pl.BlockSpec((1, MOD_ROWS, tn), lambda i, j: (i, 0, j)),
        out_shape=jax.ShapeDtypeStruct((depth, MOD_ROWS, n), F32),
        compiler_params=pltpu.CompilerParams(
            dimension_semantics=("arbitrary", "arbitrary"), vmem_limit_bytes=VMEM_LIMIT),
        name="modulation",
    )(cc, mod_w, mod_b.reshape(depth, 1, n))


def _pool_kernel(*refs, with_ctx, halo):
    if with_ctx:
        c_ref, x_ref, modc_ref, modx_ref, w_ref, s_ref, oc_ref, ox_ref, pad_ref = refs
    else:
        x_ref, modx_ref, w_ref, s_ref, ox_ref, pad_ref = refs
    g = pl.program_id(1)
    wg = w_ref[0].astype(BF16)
    scale = s_ref[...]

    def segment(src_ref, mod_ref, dst_ref, window):
        length = src_ref.shape[1]
        lo = window // 2
        hi = window - 1 - lo
        h = src_ref[0] * (1.0 + mod_ref[0, 1:2, :]) + mod_ref[0, 0:1, :]
        zeros = jnp.zeros((halo, h.shape[1]), F32)
        pad_ref[0:halo, :] = zeros
        pad_ref[halo:halo + length, :] = h
        pad_ref[halo + length:2 * halo + length, :] = zeros
        tot = h
        for off in range(-lo, hi + 1):
            if off != 0:
                tot = tot + pad_ref[halo + off:halo + off + length, :]
        t = lax.broadcasted_iota(jnp.int32, h.shape, 0)
        cnt = jnp.minimum(t + hi + 1, length) - jnp.maximum(t - lo, 0)
        p = tot / cnt.astype(F32) - h
        dst_ref[0] = (_dot(p.astype(BF16), wg) * scale).astype(dst_ref.dtype)

    for gi, window in enumerate(POOL_WINDOWS):
        @pl.when(g == gi)
        def _(window=window):
            if with_ctx:
                segment(c_ref, modc_ref, oc_ref, window)
            segment(x_ref, modx_ref, ox_ref, window)


def _pool_mixer(ctx, x, mod_all, layer, w_groups, scales, j):
    b, seq, d = x.shape
    _, ng, gd, _ = w_groups.shape
    halo = 8
    with_ctx = ctx is not None
    n_ctx = ctx.shape[1] if with_ctx else 0
    mod_spec = lambda row: pl.BlockSpec((None, 1, N_MOD, gd), lambda i, g: (layer, row(i), 0, g))
    in_specs, args = [], []
    if with_ctx:
        in_specs.append(pl.BlockSpec((1, n_ctx, gd), lambda i, g: (i, 0, g)))
        args.append(ctx)
    in_specs.append(pl.BlockSpec((1, seq, gd), lambda i, g: (i, 0, g)))
    args.append(x)
    if with_ctx:
        in_specs.append(mod_spec(lambda i: b))
        args.append(mod_all)
    in_specs += [
        mod_spec(lambda i: i),
        pl.BlockSpec((None, 1, gd, gd), lambda i, g: (j, g, 0, 0)),
        pl.BlockSpec((None, 1, gd), lambda i, g: (j, 0, g)),
    ]
    args += [mod_all, w_groups, scales.reshape(scales.shape[0], 1, d)]
    out_specs = [pl.BlockSpec((1, seq, gd), lambda i, g: (i, 0, g))]
    out_shape = [jax.ShapeDtypeStruct((b, seq, d), BF16)]
    if with_ctx:
        out_specs.insert(0, pl.BlockSpec((1, n_ctx, gd), lambda i, g: (i, 0, g)))
        out_shape.insert(0, jax.ShapeDtypeStruct((b, n_ctx, d), BF16))
    kern = functools.partial(_pool_kernel, with_ctx=with_ctx, halo=halo)
    outs = pl.pallas_call(
        kern,
        grid=(b, ng),
        in_specs=in_specs,
        out_specs=out_specs,
        out_shape=out_shape,
        scratch_shapes=[pltpu.VMEM((seq + 2 * halo, gd), F32)],
        compiler_params=pltpu.CompilerParams(
            dimension_semantics=("arbitrary", "arbitrary"), vmem_limit_bytes=VMEM_LIMIT),
        name="pool_mixer",
    )(*args)
    return (outs[0], outs[1]) if with_ctx else (None, outs[0])


def _proj_kernel(c_ref, x_ref, mod_ref, w_ref, o_ref, *, n_ctx_tiles, n_chunk, first_chunk_scale):
    def run(src_ref):
        h = (src_ref[0] * (1.0 + mod_ref[0, 1:2, :]) + mod_ref[0, 0:1, :]).astype(BF16)
        n = w_ref.shape[1]
        for c0 in range(0, n, n_chunk):
            r = _dot(h, w_ref[:, c0:c0 + n_chunk])
            if c0 == 0 and first_chunk_scale != 1.0:
                r = r * first_chunk_scale
            o_ref[0, :, c0:c0 + n_chunk] = r.astype(o_ref.dtype)

    _per_segment(pl.program_id(1), n_ctx_tiles, lambda: run(c_ref), lambda: run(x_ref))


def _per_segment(j, n_ctx_tiles, ctx_fn, latent_fn):
    pl.when(j < n_ctx_tiles)(ctx_fn)
    pl.when(j >= n_ctx_tiles)(latent_fn)


def _segment_specs(b, d, n_ctx_tiles, layer):
    return [
        pl.BlockSpec((1, TOKEN_TILE, d), lambda i, j: (i, jnp.minimum(j, n_ctx_tiles - 1), 0)),
        pl.BlockSpec((1, TOKEN_TILE, d), lambda i, j: (i, jnp.maximum(j - n_ctx_tiles, 0), 0)),
        pl.BlockSpec((None, 1, N_MOD, d), lambda i, j: (layer, jnp.where(j < n_ctx_tiles, b, i), 0, 0)),
    ]


def _modulated_proj(ctx, x, mod_all, layer, w_all, j_w, out_dtype, n_chunk, first_chunk_scale=1.0):
    b, seq, d = x.shape
    n_ctx = ctx.shape[1]
    n = w_all.shape[2]
    nct = n_ctx // TOKEN_TILE
    kern = functools.partial(_proj_kernel, n_ctx_tiles=nct, n_chunk=n_chunk,
                             first_chunk_scale=first_chunk_scale)
    return pl.pallas_call(
        kern,
        grid=(b, (n_ctx + seq) // TOKEN_TILE),
        in_specs=_segment_specs(b, d, nct, layer) + [_resident((d, n), j_w)],
        out_specs=pl.BlockSpec((1, TOKEN_TILE, n), lambda i, j: (i, j, 0)),
        out_shape=jax.ShapeDtypeStruct((b, n_ctx + seq, n), out_dtype),
        compiler_params=pltpu.CompilerParams(
            dimension_semantics=("arbitrary", "arbitrary"), vmem_limit_bytes=VMEM_LIMIT),
        name="modulated_proj",
    )(ctx, x, mod_all, w_all)


def _ffn_kernel(*refs, with_ctx, n_ctx_tiles, has_mix, ffn_hidden):
    refs = list(refs)
    if with_ctx:
        c_ref, x_ref, yc_ref, yx_ref = refs[:4]
        refs = refs[4:]
    else:
        x_ref, yx_ref = refs[:2]
        refs = refs[2:]
    mod_ref, lng_ref, lnb_ref = refs[:3]
    wmix_ref = refs[3] if has_mix else None
    win_ref, wout_ref = refs[3 + has_mix:5 + has_mix]
    outs = refs[5 + has_mix:]

    def run(src_ref, y_ref, dst_ref):
        g1 = mod_ref[0, 2:3, :]
        sh2 = mod_ref[0, 3:4, :]
        sc2 = mod_ref[0, 4:5, :]
        g2 = mod_ref[0, 5:6, :]
        for r0 in range(0, src_ref.shape[1], TOKEN_TILE):
            rows = slice(r0, r0 + TOKEN_TILE)
            x = src_ref[0, rows, :]
            if has_mix:
                ox = _dot(y_ref[0, rows, :], wmix_ref[...])
            else:
                ox = y_ref[0, rows, :]
            x1 = _layer_norm(ALPHA * x + g1 * ox, lng_ref[0:1, :], lnb_ref[0:1, :])
            h2 = (x1 * (1.0 + sc2) + sh2).astype(BF16)
            gu = _dot(h2, win_ref[...])
            act = (_silu(gu[:, :ffn_hidden]) * gu[:, ffn_hidden:]).astype(BF16)
            y2 = _dot(act, wout_ref[...])
            dst_ref[0, rows, :] = _layer_norm(ALPHA * x1 + g2 * y2, lng_ref[1:2, :], lnb_ref[1:2, :])

    if with_ctx:
        _per_segment(pl.program_id(1), n_ctx_tiles,
                     lambda: run(c_ref, yc_ref, outs[0]), lambda: run(x_ref, yx_ref, outs[1]))
    else:
        run(x_ref, yx_ref, outs[0])


def _ffn_block(ctx, x, y_ctx, y_x, mod_all, layer, ln_g, ln_b, w_mix_all, j_mix, w_in_all, w_out_all):
    b, seq, d = x.shape
    f = w_out_all.shape[1]
    with_ctx = ctx is not None
    nct = ctx.shape[1] // TOKEN_TILE if with_ctx else 0
    has_mix = w_mix_all is not None
    step = FFN_STEP_TILES * TOKEN_TILE
    assert seq % step == 0
    c_map = lambda i, j: (i, jnp.minimum(j, nct - 1), 0)
    x_map = lambda i, j: (i, jnp.maximum(j - nct, 0), 0)
    c_spec = pl.BlockSpec((1, TOKEN_TILE, d), c_map)
    x_spec = pl.BlockSpec((1, step, d), x_map)
    if with_ctx:
        in_specs = [c_spec, x_spec, c_spec, x_spec,
                    pl.BlockSpec((None, 1, N_MOD, d), lambda i, j: (layer, jnp.where(j < nct, b, i), 0, 0))]
        args = [ctx, x, y_ctx, y_x, mod_all]
        out_specs = [c_spec, x_spec]
        out_shape = [jax.ShapeDtypeStruct(ctx.shape, F32), jax.ShapeDtypeStruct(x.shape, F32)]
    else:
        in_specs = [x_spec, x_spec, pl.BlockSpec((None, 1, N_MOD, d), lambda i, j: (layer, i, 0, 0))]
        args = [x, y_x, mod_all]
        out_specs = [x_spec]
        out_shape = [jax.ShapeDtypeStruct(x.shape, F32)]
    in_specs += [_resident((2, d), layer), _resident((2, d), layer)]
    args += [ln_g, ln_b]
    if has_mix:
        in_specs.append(_resident((d, d), j_mix))
        args.append(w_mix_all)
    in_specs += [_resident((d, 2 * f), layer), _resident((f, d), layer)]
    args += [w_in_all, w_out_all]
    kern = functools.partial(_ffn_kernel, with_ctx=with_ctx, n_ctx_tiles=nct, has_mix=has_mix, ffn_hidden=f)
    outs = pl.pallas_call(
        kern,
        grid=(b, nct + seq // step),
        in_specs=in_specs,
        out_specs=out_specs,
        out_shape=out_shape,
        compiler_params=pltpu.CompilerParams(
            dimension_semantics=("arbitrary", "arbitrary"), vmem_limit_bytes=VMEM_LIMIT),
        name="ffn_block",
    )(*args)
    return (outs[0], outs[1]) if with_ctx else (None, outs[0])


def _hgrn_constants():
    c = HGRN_CHUNK
    t = np.arange(c)[:, None]
    u = np.arange(c)[None, :]
    mats = [u <= t]
    masks = []
    m = c // 2
    while m >= 1:
        ref = (t // (2 * m)) * 2 * m + m - 1
        second = (t % (2 * m)) >= m
        mats.append(np.where(second, (u > ref) & (u <= t), (u > t) & (u <= ref)))
        masks.append(((t // (2 * m)) == (u // (2 * m))) & second & ((u % (2 * m)) < m))
        m //= 2
    mats.append(u > t)
    masks += [np.broadcast_to(t == u, (c, c)), np.zeros((c, c), bool)]
    pair = lambda ms: np.stack([np.concatenate([ms[2 * p], ms[2 * p + 1]], 1) for p in range(len(ms) // 2)])
    flip = lambda ms: [a[::-1, ::-1] for a in ms]
    dmat = np.stack([np.concatenate(mats, 0), np.concatenate(flip(mats), 0)])
    dmat = np.concatenate([dmat] * 3, axis=2)
    mask = np.stack([pair(masks), pair(flip(masks))])
    return dmat.astype(np.float32), mask.astype(np.float32)


def _hgrn_scan_kernel(q_ref, v_ref, zf_ref, zb_ref, g_ref, lbl_ref, nw_ref, dm_ref, mk_ref,
                      yc_ref, yx_ref, o_ref, oi_ref, qx_ref, ke_ref, r_ref, qf_ref, kf_ref, x_ref, u_ref, dec_ref,
                      *, layer, n_ctx):
    c = HGRN_CHUNK
    hd = HGRN_HEAD_DIM
    t_all = q_ref.shape[1]
    n_chunks = t_all // c
    n_ctx_chunks = n_ctx // c
    n_pairs = mk_ref.shape[1]
    n_blocks = dm_ref.shape[1] // c

    def lower_bound(direction):
        rows = [lbl_ref[direction, j:j + 1, :] for j in range(lbl_ref.shape[1])]
        mx = functools.reduce(jnp.maximum, rows)
        e = [jnp.exp(r - mx) for r in rows]
        tot = functools.reduce(lambda a, b_: a + b_, e)
        p = [ei / tot for ei in e]
        acc = p[0]
        for j in range(1, layer + 1):
            acc = acc + p[j]
        return acc - p[0]

    lb = (lower_bound(0), lower_bound(1))
    z_refs = (zf_ref, zb_ref)
    end_row = (c - 1, 0)
    zeros_k = jnp.zeros((c, hd), BF16)

    def decays(i, slot):
        rows2 = pl.ds(pl.multiple_of(i * 2 * c, 2 * c), 2 * c)
        q = _silu(q_ref[0, rows2, :])
        qf_ref[slot] = q.astype(BF16)
        for d in range(2):
            f = lb[d] + (1.0 - lb[d]) * _sigmoid(z_refs[d][0, rows2, :])
            k = 1.0 - f
            kf_ref[slot, d] = k.astype(BF16)
            g = jnp.log(f) * LOG2E
            hi = g.astype(BF16)
            r1 = g - hi.astype(F32)
            mid = r1.astype(BF16)
            lo = (r1 - mid.astype(F32)).astype(BF16)
            side = lambda a: jnp.concatenate([a[:c], a[c:]], axis=1)
            x2 = jnp.exp2(_dot(dm_ref[d], jnp.concatenate([side(hi), side(mid), side(lo)], axis=0)))
            for cc in range(2):
                xs = x2[:, cc * hd:(cc + 1) * hd]
                rows = pl.ds(pl.multiple_of(i * 2 * c + cc * c, c), c)
                x_ref[slot, d, cc] = xs[c:(n_blocks - 1) * c].astype(BF16)
                qx_ref[d, rows, :] = (q[cc * c:(cc + 1) * c] * xs[0:c]).astype(BF16)
                ke_ref[d, rows, :] = (k[cc * c:(cc + 1) * c] * xs[(n_blocks - 1) * c:]).astype(BF16)
                dec_ref[d, 2 * i + cc] = jnp.broadcast_to(xs[end_row[d]:end_row[d] + 1], (8, hd))

    def interactions(i, slot):
        for cc in range(2):
            rows = pl.ds(pl.multiple_of(i * 2 * c + cc * c, c), c)
            qc = qf_ref[slot, cc * c:(cc + 1) * c, :]
            for d in range(2):
                kc = kf_ref[slot, d, cc * c:(cc + 1) * c, :]
                xs = x_ref[slot, d, cc]
                diag = jnp.sum(qc.astype(F32) * kc.astype(F32), axis=-1, keepdims=True)
                r = mk_ref[d, n_pairs - 1] * diag
                for p in range(n_pairs - 1):
                    xa = xs[2 * p * c:(2 * p + 1) * c]
                    xb = xs[(2 * p + 1) * c:(2 * p + 2) * c]
                    lhs = jnp.concatenate([qc * xa, qc * xb], axis=1)
                    rhs = jnp.concatenate([jnp.concatenate([kc * xa, zeros_k], axis=1),
                                           jnp.concatenate([zeros_k, kc * xb], axis=1)], axis=0)
                    r = r + mk_ref[d, p] * _dot_nt(lhs, rhs)
                r_ref[d, rows, :] = r.astype(BF16)

    def values(ci, carry):
        rows = pl.ds(pl.multiple_of(ci * c, c), c)
        vc = v_ref[0, rows, :].astype(BF16)
        oo = _dot(jnp.concatenate([r_ref[0, rows, :], r_ref[1, rows, :]], axis=0),
                  jnp.concatenate([vc, vc], axis=0))
        o_ref[0, rows, :] = oo[:c]
        o_ref[1, rows, :] = oo[c:]
        uu = _dot_tn(vc, jnp.concatenate([ke_ref[0, rows, :], ke_ref[1, rows, :]], axis=1))
        u_ref[0, ci] = uu[:, :hd]
        u_ref[1, ci] = uu[:, hd:]
        return carry

    n_cpairs = n_chunks // 2

    def pipelined(k, carry):
        decays(2 * k + 1, 1)
        interactions(2 * k, 0)
        decays(2 * k + 2, 0)
        interactions(2 * k + 1, 1)
        return carry

    decays(0, 0)
    lax.fori_loop(0, n_cpairs // 2 - 1, pipelined, 0, unroll=True)
    decays(n_cpairs - 1, 1)
    interactions(n_cpairs - 2, 0)
    interactions(n_cpairs - 1, 1)
    lax.fori_loop(0, n_chunks, values, 0, unroll=True)

    def inter(i, states):
        cb = jnp.where(i < n_ctx_chunks, n_ctx_chunks - 1 - i, n_chunks - 1 - (i - n_ctx_chunks))
        new_states = []
        for d, ci in ((0, i), (1, cb)):
            rows = pl.ds(pl.multiple_of(ci * c, c), c)
            st = states[d]
            oi_ref[d, rows, :] = _dot_nt(qx_ref[d, rows, :], st.astype(BF16))
            new_states.append(st * dec_ref[d, ci][0:1] + u_ref[d, ci])
        return tuple(new_states)

    zero_state = jnp.zeros((hd, hd), F32)
    lax.fori_loop(0, n_chunks, inter, (zero_state, zero_state), unroll=True)

    nw = nw_ref[...]

    def finish(dst_ref, first_row):
        def tile(j, carry):
            rows = pl.ds(pl.multiple_of(first_row + j * TOKEN_TILE, TOKEN_TILE), TOKEN_TILE)
            o = (o_ref[0, rows, :] + oi_ref[0, rows, :]) + (o_ref[1, rows, :] + oi_ref[1, rows, :])
            o = o * lax.rsqrt(jnp.mean(o * o, axis=-1, keepdims=True) + RMS_EPS)
            gate = _silu(g_ref[0, rows, :])
            out_rows = pl.ds(pl.multiple_of(j * TOKEN_TILE, TOKEN_TILE), TOKEN_TILE)
            dst_ref[0, out_rows, :] = (o * nw * gate).astype(dst_ref.dtype)
            return carry

        lax.fori_loop(0, dst_ref.shape[1] // TOKEN_TILE, tile, 0, unroll=4)

    finish(yc_ref, 0)
    finish(yx_ref, n_ctx)


def _hgrn_scan(proj, lb_logits, norm_w_all, j_w, *, layer, n_ctx):
    b, t, n = proj.shape
    d = n // 5
    hd = HGRN_HEAD_DIM
    heads = d // hd
    n_chunks = t // HGRN_CHUNK
    assert n_chunks % 4 == 0
    dmat, mask = _hgrn_constants()
    part = lambda p: pl.BlockSpec((1, t, hd), lambda i, h: (i, 0, p * heads + h))
    kern = functools.partial(_hgrn_scan_kernel, layer=layer, n_ctx=n_ctx)
    return pl.pallas_call(
        kern,
        grid=(b, heads),
        in_specs=[
            part(0), part(1), part(2), part(3), part(4),
            pl.BlockSpec((2, lb_logits.shape[1], hd), lambda i, h: (0, 0, h)),
            pl.BlockSpec((None, 1, hd), lambda i, h: (j_w, 0, h)),
            _resident(dmat.shape),
            _resident(mask.shape),
        ],
        out_specs=[pl.BlockSpec((1, n_ctx, hd), lambda i, h: (i, 0, h)),
                   pl.BlockSpec((1, t - n_ctx, hd), lambda i, h: (i, 0, h))],
        out_shape=[jax.ShapeDtypeStruct((b, n_ctx, d), BF16),
                   jax.ShapeDtypeStruct((b, t - n_ctx, d), BF16)],
        scratch_shapes=[
            pltpu.VMEM((2, t, hd), F32),
            pltpu.VMEM((2, t, hd), F32),
            pltpu.VMEM((2, t, hd), BF16),
            pltpu.VMEM((2, t, hd), BF16),
            pltpu.VMEM((2, t, hd), BF16),
            pltpu.VMEM((2, 2 * HGRN_CHUNK, hd), BF16),
            pltpu.VMEM((2, 2, 2 * HGRN_CHUNK, hd), BF16),
            pltpu.VMEM((2, 2, 2, (dmat.shape[1] // HGRN_CHUNK - 2) * HGRN_CHUNK, hd), BF16),
            pltpu.VMEM((2, n_chunks, hd, hd), F32),
            pltpu.VMEM((2, n_chunks, 8, hd), F32),
        ],
        compiler_params=pltpu.CompilerParams(
            dimension_semantics=("arbitrary", "arbitrary"), vmem_limit_bytes=VMEM_LIMIT),
        name="hgrn_scan",
    )(proj, proj, proj, proj, proj, lb_logits, norm_w_all.reshape(norm_w_all.shape[0], 1, d),
      jnp.asarray(dmat, BF16), jnp.asarray(mask, F32))


def _na_bias_table(rpb):
    qcol = np.arange(GRID_W)[:, None]
    kcol = np.arange(GRID_W)[None, :]
    win = np.clip(qcol - NA_COLS // 2, 0, GRID_W - NA_COLS)
    ok = (kcol >= win) & (kcol < win + NA_COLS)
    dcol = np.arange(2 * NA_COLS - 1)[:, None, None]
    select = ((kcol - qcol + NA_COLS - 1)[None] == dcol) & ok[None]
    per_row = jnp.einsum('hrc,cqk->hrqk', rpb.astype(F32), jnp.asarray(select, F32),
                         precision=lax.Precision.HIGHEST)
    per_row = jnp.where(jnp.asarray(ok)[None, None], per_row * LOG2E, NEG_BIG)
    return jnp.concatenate([per_row[:, :-1], per_row[:, 1:]], axis=-1)


def _na_kernel(q_ref, k_ref, v_ref, bias_ref, yc_ref, yx_ref, qs_ref, s_ref, p_ref, l_ref, *, n_ctx, rows):
    w = GRID_W
    kr = min(NA_ROWS, rows)
    n_loc = kr * w
    grp = NA_ROW_GROUP
    lanes = 2 * NA_HEAD_DIM
    first = lax.broadcasted_iota(jnp.int32, (1, lanes), 1) < NA_HEAD_DIM

    def split_heads(qv):
        zero = jnp.zeros_like(qv)
        return jnp.concatenate([jnp.where(first, qv, zero), jnp.where(first, zero, qv)], axis=0)

    def merge_heads(o, n):
        return jnp.where(first, o[:n], o[n:])

    def window(r):
        rs = jnp.clip(r - kr // 2, 0, rows - kr)
        return r - rs, pl.ds(pl.multiple_of(n_ctx + rs * w, w), n_loc)

    kc = k_ref[0, 0:n_ctx, :]
    vc = v_ref[0, 0:n_ctx, :]

    s = _dot_nt(split_heads(q_ref[0, 0:n_ctx, :]), kc)
    p = jnp.exp2(s - jnp.max(s, axis=-1, keepdims=True))
    o = _dot(p.astype(BF16), vc) / jnp.sum(p, axis=-1, keepdims=True)
    yc_ref[0] = merge_heads(o, n_ctx).astype(yc_ref.dtype)

    def prep(r, carry):
        qs_ref[r] = split_heads(q_ref[0, pl.ds(pl.multiple_of(n_ctx + r * w, w), w), :])
        return carry

    def scores(g, slot):
        for rr in range(grp):
            case, keys = window(g * grp + rr)
            head_bias = lambda hh: jnp.concatenate(
                [bias_ref[hh, NA_ROWS - 1 - case + a] for a in range(0, kr, 2)], axis=1)
            bias = jnp.concatenate([head_bias(0), head_bias(1)], axis=0)
            s_ref[slot, rr, :, 0:n_loc] = _dot_nt(qs_ref[g * grp + rr], k_ref[0, keys, :]) + bias
        qg = qs_ref[pl.ds(g * grp, grp)].reshape(grp * 2 * w, lanes)
        s_ref[slot, :, :, n_loc:] = _dot_nt(qg, kc).reshape(grp, 2 * w, n_ctx)

    def softmax(slot):
        for rr in range(grp):
            sr = s_ref[slot, rr]
            tiles = [sr[:, i * lanes:(i + 1) * lanes] for i in range(sr.shape[1] // lanes)]
            m = jnp.max(functools.reduce(jnp.maximum, tiles), axis=-1, keepdims=True)
            pr = [jnp.exp2(tile - m) for tile in tiles]
            den = jnp.sum(functools.reduce(lambda u, v_: u + v_, pr), axis=-1, keepdims=True)
            l_ref[slot, rr] = jnp.broadcast_to(den, (2 * w, lanes))
            p_ref[slot, rr] = jnp.concatenate(pr, axis=1).astype(BF16)

    def values(g, slot):
        pg = p_ref[slot, :, :, n_loc:].reshape(grp * 2 * w, n_ctx)
        o_ctx = _dot(pg, vc).reshape(grp, 2 * w, lanes)
        for rr in range(grp):
            _, keys = window(g * grp + rr)
            orow = (_dot(p_ref[slot, rr, :, 0:n_loc], v_ref[0, keys, :]) + o_ctx[rr]) / l_ref[slot, rr]
            out_rows = pl.ds(pl.multiple_of((g * grp + rr) * w, w), w)
            yx_ref[0, out_rows, :] = merge_heads(orow, w).astype(yx_ref.dtype)

    n_groups = rows // grp

    def pipelined(k, carry):
        g = 2 * k
        scores(g + 2, 0)
        softmax(1)
        values(g, 0)
        scores(g + 3, 1)
        softmax(0)
        values(g + 1, 1)
        return carry

    lax.fori_loop(0, rows, prep, 0, unroll=8)
    scores(0, 0)
    scores(1, 1)
    softmax(0)
    lax.fori_loop(0, n_groups // 2 - 1, pipelined, 0, unroll=True)
    softmax(1)
    values(n_groups - 2, 0)
    values(n_groups - 1, 1)


def _na_attention(qkv, bias, *, n_ctx):
    b, t, n = qkv.shape
    d = n // 3
    lanes = 2 * NA_HEAD_DIM
    pairs = d // lanes
    rows = (t - n_ctx) // GRID_W
    assert rows % NA_ROW_GROUP == 0 and rows >= NA_ROWS and NA_ROWS % 2 == 0
    n_keys = NA_ROWS * GRID_W + n_ctx
    part = lambda p: pl.BlockSpec((1, t, lanes), lambda h, i: (i, 0, p * pairs + h))
    kern = functools.partial(_na_kernel, n_ctx=n_ctx, rows=rows)
    return pl.pallas_call(
        kern,
        grid=(pairs, b),
        in_specs=[
            part(0), part(1), part(2),
            pl.BlockSpec((2,) + bias.shape[1:], lambda h, i: (h, 0, 0, 0)),
        ],
        out_specs=[pl.BlockSpec((1, n_ctx, lanes), lambda h, i: (i, 0, h)),
                   pl.BlockSpec((1, t - n_ctx, lanes), lambda h, i: (i, 0, h))],
        out_shape=[jax.ShapeDtypeStruct((b, n_ctx, d), BF16),
                   jax.ShapeDtypeStruct((b, t - n_ctx, d), BF16)],
        scratch_shapes=[
            pltpu.VMEM((rows, 2 * GRID_W, lanes), BF16),
            pltpu.VMEM((2, NA_ROW_GROUP, 2 * GRID_W, n_keys), F32),
            pltpu.VMEM((2, NA_ROW_GROUP, 2 * GRID_W, n_keys), BF16),
            pltpu.VMEM((2, NA_ROW_GROUP, 2 * GRID_W, lanes), F32),
        ],
        compiler_params=pltpu.CompilerParams(
            dimension_semantics=("arbitrary", "arbitrary"), vmem_limit_bytes=VMEM_LIMIT),
        name="na_attention",
    )(qkv, qkv, qkv, bias)


def kernel(x, c, ctx, c_ctx, mod_w, mod_b, ln_g, ln_b, ffn_w_in, ffn_w_out, pool_w, pool_scale,
           hgrn_w_in, hgrn_lb_logits, hgrn_norm_w, hgrn_w_out, na_w_qkv, na_rpb, na_w_out):
    b, seq, d = x.shape
    n_ctx = ctx.shape[1]
    depth = mod_w.shape[0]
    assert n_ctx % TOKEN_TILE == 0 and seq % TOKEN_TILE == 0 and b < MOD_ROWS
    assert seq % GRID_W == 0 and n_ctx % HGRN_CHUNK == 0 and seq % HGRN_CHUNK == 0

    cc = jnp.concatenate([c, c_ctx[None, :], jnp.zeros((MOD_ROWS - b - 1, d), c.dtype)], axis=0)
    mod_all = _modulation(cc, mod_w, mod_b).reshape(depth, MOD_ROWS, N_MOD, d)
    w_in_all = ffn_w_in.astype(BF16)
    w_out_all = ffn_w_out.astype(BF16)
    hgrn_w_in_all = hgrn_w_in.astype(BF16)
    hgrn_w_out_all = hgrn_w_out.astype(BF16)
    na_w_qkv_all = na_w_qkv.astype(BF16)
    na_w_out_all = na_w_out.astype(BF16)

    for i in range(depth):
        last = i == depth - 1
        kind = i % N_MIXERS
        j = i // N_MIXERS
        w_mix_all = None
        if kind == 0:
            y_ctx, y_x = _pool_mixer(None if last else ctx, x, mod_all, i, pool_w, pool_scale, j)
        elif kind == 1:
            proj = _modulated_proj(ctx, x, mod_all, i, hgrn_w_in_all, j, F32, n_chunk=d)
            y_ctx, y_x = _hgrn_scan(proj, hgrn_lb_logits, hgrn_norm_w, j, layer=i, n_ctx=n_ctx)
            w_mix_all = hgrn_w_out_all
        else:
            qkv = _modulated_proj(ctx, x, mod_all, i, na_w_qkv_all, j, BF16, n_chunk=d,
                                  first_chunk_scale=NA_HEAD_DIM ** -0.5 * LOG2E)
            y_ctx, y_x = _na_attention(qkv, _na_bias_table(na_rpb[j]), n_ctx=n_ctx)
            w_mix_all = na_w_out_all
        ctx, x = _ffn_block(None if last else ctx, x, y_ctx, y_x, mod_all, i, ln_g, ln_b,
                            w_mix_all, j, w_in_all, w_out_all)
    return x
```

```python
import functools

import numpy as np
import jax
import jax.numpy as jnp
from jax import lax
from jax.experimental import pallas as pl
from jax.experimental.pallas import tpu as pltpu

DEPTH = 4
N_MIXERS = 3
N_MOD = 6
ALPHA = (2 * DEPTH) ** 0.25
LN_EPS = 1e-6
RMS_EPS = 1e-6
POOL_WINDOWS = (2, 4, 8, 16)
HGRN_HEAD_DIM = 128
HGRN_CHUNK = 64
NA_HEAD_DIM = 64
NA_ROWS = 8
NA_COLS = 16
GRID_W = 64
NEG_BIG = -1e30
LOG2E = 1.4426950408889634
NA_ROW_GROUP = 4

TOKEN_TILE = 256
FFN_STEP_TILES = 2
MOD_ROWS = 16
VMEM_LIMIT = 56 * 1024 * 1024

F32 = jnp.float32
BF16 = jnp.bfloat16


def _sigmoid(x):
    return 1.0 / (1.0 + jnp.exp(-x))


def _silu(x):
    return x * _sigmoid(x)


def _layer_norm(v, g, b):
    mu = jnp.mean(v, axis=-1, keepdims=True)
    d = v - mu
    var = jnp.mean(d * d, axis=-1, keepdims=True)
    return d * lax.rsqrt(var + LN_EPS) * g + b


def _dot(a, b):
    return jnp.dot(a, b, preferred_element_type=F32)


def _dot_nt(a, b):
    return lax.dot_general(a, b, (((1,), (1,)), ((), ())), preferred_element_type=F32)


def _dot_tn(a, b):
    return lax.dot_general(a, b, (((0,), (0,)), ((), ())), preferred_element_type=F32)


def _resident(shape, layer=None):
    zeros = (0,) * len(shape)
    if layer is None:
        return pl.BlockSpec(shape, lambda *_: zeros, pipeline_mode=pl.Buffered(1))
    return pl.BlockSpec((None,) + tuple(shape), lambda *_: (layer,) + zeros, pipeline_mode=pl.Buffered(1))


def _mod_kernel(c_ref, w_ref, b_ref, o_ref):
    s = _silu(c_ref[...]).astype(BF16)
    o_ref[0] = _dot(s, w_ref[0].astype(BF16)) + b_ref[0]


def _modulation(cc, mod_w, mod_b):
    depth, d, n = mod_w.shape
    tn = n // 4
    return pl.pallas_call(
        _mod_kernel,
        grid=(depth, n // tn),
        in_specs=[
            pl.BlockSpec((MOD_ROWS, d), lambda i, j: (0, 0)),
            pl.BlockSpec((1, d, tn), lambda i, j: (i, 0, j)),
            pl.BlockSpec((1, 1, tn), lambda i, j: (i, 0, j)),
        ],
        out_specs=pl.BlockSpec((1, MOD_ROWS, tn), lambda i, j: (i, 0, j)),
        out_shape=jax.ShapeDtypeStruct((depth, MOD_ROWS, n), F32),
        compiler_params=pltpu.CompilerParams(
            dimension_semantics=("arbitrary", "arbitrary"), vmem_limit_bytes=VMEM_LIMIT),
        name="modulation",
    )(cc, mod_w, mod_b.reshape(depth, 1, n))


def _pool_kernel(*refs, with_ctx, halo):
    if with_ctx:
        c_ref, x_ref, modc_ref, modx_ref, w_ref, s_ref, oc_ref, ox_ref, pad_ref = refs
    else:
        x_ref, modx_ref, w_ref, s_ref, ox_ref, pad_ref = refs
    g = pl.program_id(1)
    wg = w_ref[0].astype(BF16)
    scale = s_ref[...]

    def segment(src_ref, mod_ref, dst_ref, window):
        length = src_ref.shape[1]
        lo = window // 2
        hi = window - 1 - lo
        h = src_ref[0] * (1.0 + mod_ref[0, 1:2, :]) + mod_ref[0, 0:1, :]
        zeros = jnp.zeros((halo, h.shape[1]), F32)
        pad_ref[0:halo, :] = zeros
        pad_ref[halo:halo + length, :] = h
        pad_ref[halo + length:2 * halo + length, :] = zeros
        tot = h
        for off in range(-lo, hi + 1):
            if off != 0:
                tot = tot + pad_ref[halo + off:halo + off + length, :]
        t = lax.broadcasted_iota(jnp.int32, h.shape, 0)
        cnt = jnp.minimum(t + hi + 1, length) - jnp.maximum(t - lo, 0)
        p = tot / cnt.astype(F32) - h
        dst_ref[0] = (_dot(p.astype(BF16), wg) * scale).astype(dst_ref.dtype)

    for gi, window in enumerate(POOL_WINDOWS):
        @pl.when(g == gi)
        def _(window=window):
            if with_ctx:
                segment(c_ref, modc_ref, oc_ref, window)
            segment(x_ref, modx_ref, ox_ref, window)


def _pool_mixer(ctx, x, mod_all, layer, w_groups, scales, j):
    b, seq, d = x.shape
    _, ng, gd, _ = w_groups.shape
    halo = 8
    with_ctx = ctx is not None
    n_ctx = ctx.shape[1] if with_ctx else 0
    mod_spec = lambda row: pl.BlockSpec((None, 1, N_MOD, gd), lambda i, g: (layer, row(i), 0, g))
    in_specs, args = [], []
    if with_ctx:
        in_specs.append(pl.BlockSpec((1, n_ctx, gd), lambda i, g: (i, 0, g)))
        args.append(ctx)
    in_specs.append(pl.BlockSpec((1, seq, gd), lambda i, g: (i, 0, g)))
    args.append(x)
    if with_ctx:
        in_specs.append(mod_spec(lambda i: b))
        args.append(mod_all)
    in_specs += [
        mod_spec(lambda i: i),
        pl.BlockSpec((None, 1, gd, gd), lambda i, g: (j, g, 0, 0)),
        pl.BlockSpec((None, 1, gd), lambda i, g: (j, 0, g)),
    ]
    args += [mod_all, w_groups, scales.reshape(scales.shape[0], 1, d)]
    out_specs = [pl.BlockSpec((1, seq, gd), lambda i, g: (i, 0, g))]
    out_shape = [jax.ShapeDtypeStruct((b, seq, d), F32)]
    if with_ctx:
        out_specs.insert(0, pl.BlockSpec((1, n_ctx, gd), lambda i, g: (i, 0, g)))
        out_shape.insert(0, jax.ShapeDtypeStruct((b, n_ctx, d), F32))
    kern = functools.partial(_pool_kernel, with_ctx=with_ctx, halo=halo)
    outs = pl.pallas_call(
        kern,
        grid=(b, ng),
        in_specs=in_specs,
        out_specs=out_specs,
        out_shape=out_shape,
        scratch_shapes=[pltpu.VMEM((seq + 2 * halo, gd), F32)],
        compiler_params=pltpu.CompilerParams(
            dimension_semantics=("arbitrary", "arbitrary"), vmem_limit_bytes=VMEM_LIMIT),
        name="pool_mixer",
    )(*args)
    return (outs[0], outs[1]) if with_ctx else (None, outs[0])


def _proj_kernel(c_ref, x_ref, mod_ref, w_ref, o_ref, *, n_ctx_tiles, n_chunk, first_chunk_scale):
    def run(src_ref):
        h = (src_ref[0] * (1.0 + mod_ref[0, 1:2, :]) + mod_ref[0, 0:1, :]).astype(BF16)
        n = w_ref.shape[1]
        for c0 in range(0, n, n_chunk):
            r = _dot(h, w_ref[:, c0:c0 + n_chunk])
            if c0 == 0 and first_chunk_scale != 1.0:
                r = r * first_chunk_scale
            o_ref[0, :, c0:c0 + n_chunk] = r.astype(o_ref.dtype)

    _per_segment(pl.program_id(1), n_ctx_tiles, lambda: run(c_ref), lambda: run(x_ref))


def _per_segment(j, n_ctx_tiles, ctx_fn, latent_fn):
    pl.when(j < n_ctx_tiles)(ctx_fn)
    pl.when(j >= n_ctx_tiles)(latent_fn)


def _segment_specs(b, d, n_ctx_tiles, layer):
    return [
        pl.BlockSpec((1, TOKEN_TILE, d), lambda i, j: (i, jnp.minimum(j, n_ctx_tiles - 1), 0)),
        pl.BlockSpec((1, TOKEN_TILE, d), lambda i, j: (i, jnp.maximum(j - n_ctx_tiles, 0), 0)),
        pl.BlockSpec((None, 1, N_MOD, d), lambda i, j: (layer, jnp.where(j < n_ctx_tiles, b, i), 0, 0)),
    ]


def _modulated_proj(ctx, x, mod_all, layer, w_all, j_w, out_dtype, n_chunk, first_chunk_scale=1.0):
    b, seq, d = x.shape
    n_ctx = ctx.shape[1]
    n = w_all.shape[2]
    nct = n_ctx // TOKEN_TILE
    kern = functools.partial(_proj_kernel, n_ctx_tiles=nct, n_chunk=n_chunk,
                             first_chunk_scale=first_chunk_scale)
    return pl.pallas_call(
        kern,
        grid=(b, (n_ctx + seq) // TOKEN_TILE),
        in_specs=_segment_specs(b, d, nct, layer) + [_resident((d, n), j_w)],
        out_specs=pl.BlockSpec((1, TOKEN_TILE, n), lambda i, j: (i, j, 0)),
        out_shape=jax.ShapeDtypeStruct((b, n_ctx + seq, n), out_dtype),
        compiler_params=pltpu.CompilerParams(
            dimension_semantics=("arbitrary", "arbitrary"), vmem_limit_bytes=VMEM_LIMIT),
        name="modulated_proj",
    )(ctx, x, mod_all, w_all)


def _ffn_kernel(*refs, with_ctx, n_ctx_tiles, has_mix, ffn_hidden):
    refs = list(refs)
    if with_ctx:
        c_ref, x_ref, yc_ref, yx_ref = refs[:4]
        refs = refs[4:]
    else:
        x_ref, yx_ref = refs[:2]
        refs = refs[2:]
    mod_ref, lng_ref, lnb_ref = refs[:3]
    wmix_ref = refs[3] if has_mix else None
    win_ref, wout_ref = refs[3 + has_mix:5 + has_mix]
    outs = refs[5 + has_mix:]

    def run(src_ref, y_ref, dst_ref):
        g1 = mod_ref[0, 2:3, :]
        sh2 = mod_ref[0, 3:4, :]
        sc2 = mod_ref[0, 4:5, :]
        g2 = mod_ref[0, 5:6, :]
        for r0 in range(0, src_ref.shape[1], TOKEN_TILE):
            rows = slice(r0, r0 + TOKEN_TILE)
            x = src_ref[0, rows, :]
            if has_mix:
                ox = _dot(y_ref[0, rows, :], wmix_ref[...])
            else:
                ox = y_ref[0, rows, :]
            x1 = _layer_norm(ALPHA * x + g1 * ox, lng_ref[0:1, :], lnb_ref[0:1, :])
            h2 = (x1 * (1.0 + sc2) + sh2).astype(BF16)
            gu = _dot(h2, win_ref[...])
            act = (_silu(gu[:, :ffn_hidden]) * gu[:, ffn_hidden:]).astype(BF16)
            y2 = _dot(act, wout_ref[...].astype(BF16))
            dst_ref[0, rows, :] = _layer_norm(ALPHA * x1 + g2 * y2, lng_ref[1:2, :], lnb_ref[1:2, :])

    if with_ctx:
        _per_segment(pl.program_id(1), n_ctx_tiles,
                     lambda: run(c_ref, yc_ref, outs[0]), lambda: run(x_ref, yx_ref, outs[1]))
    else:
        run(x_ref, yx_ref, outs[0])


def _ffn_block(ctx, x, y_ctx, y_x, mod_all, layer, ln_g, ln_b, w_mix_all, j_mix, w_in_all, w_out_all):
    b, seq, d = x.shape
    f = w_out_all.shape[1]
    with_ctx = ctx is not None
    nct = ctx.shape[1] // TOKEN_TILE if with_ctx else 0
    has_mix = w_mix_all is not None
    step = FFN_STEP_TILES * TOKEN_TILE
    assert seq % step == 0
    c_map = lambda i, j: (i, jnp.minimum(j, nct - 1), 0)
    x_map = lambda i, j: (i, jnp.maximum(j - nct, 0), 0)
    c_spec = pl.BlockSpec((1, TOKEN_TILE, d), c_map)
    x_spec = pl.BlockSpec((1, step, d), x_map)
    if with_ctx:
        in_specs = [c_spec, x_spec, c_spec, x_spec,
                    pl.BlockSpec((None, 1, N_MOD, d), lambda i, j: (layer, jnp.where(j < nct, b, i), 0, 0))]
        args = [ctx, x, y_ctx, y_x, mod_all]
        out_specs = [c_spec, x_spec]
        out_shape = [jax.ShapeDtypeStruct(ctx.shape, F32), jax.ShapeDtypeStruct(x.shape, F32)]
    else:
        in_specs = [x_spec, x_spec, pl.BlockSpec((None, 1, N_MOD, d), lambda i, j: (layer, i, 0, 0))]
        args = [x, y_x, mod_all]
        out_specs = [x_spec]
        out_shape = [jax.ShapeDtypeStruct(x.shape, F32)]
    in_specs += [_resident((2, d), layer), _resident((2, d), layer)]
    args += [ln_g, ln_b]
    if has_mix:
        in_specs.append(_resident((d, d), j_mix))
        args.append(w_mix_all)
    in_specs += [_resident((d, 2 * f), layer), _resident((f, d), layer)]
    args += [w_in_all, w_out_all]
    kern = functools.partial(_ffn_kernel, with_ctx=with_ctx, n_ctx_tiles=nct, has_mix=has_mix, ffn_hidden=f)
    outs = pl.pallas_call(
        kern,
        grid=(b, nct + seq // step),
        in_specs=in_specs,
        out_specs=out_specs,
        out_shape=out_shape,
        compiler_params=pltpu.CompilerParams(
            dimension_semantics=("arbitrary", "arbitrary"), vmem_limit_bytes=VMEM_LIMIT),
        name="ffn_block",
    )(*args)
    return (outs[0], outs[1]) if with_ctx else (None, outs[0])


def _hgrn_constants():
    c = HGRN_CHUNK
    t = np.arange(c)[:, None]
    u = np.arange(c)[None, :]
    mats = [u <= t]
    masks = []
    m = c // 2
    while m >= 1:
        ref = (t // (2 * m)) * 2 * m + m - 1
        second = (t % (2 * m)) >= m
        mats.append(np.where(second, (u > ref) & (u <= t), (u > t) & (u <= ref)))
        masks.append(((t // (2 * m)) == (u // (2 * m))) & second & ((u % (2 * m)) < m))
        m //= 2
    mats.append(u > t)
    masks += [np.broadcast_to(t == u, (c, c)), np.zeros((c, c), bool)]
    pair = lambda ms: np.stack([np.concatenate([ms[2 * p], ms[2 * p + 1]], 1) for p in range(len(ms) // 2)])
    flip = lambda ms: [a[::-1, ::-1] for a in ms]
    dmat = np.stack([np.concatenate(mats, 0), np.concatenate(flip(mats), 0)])
    dmat = np.concatenate([dmat] * 3, axis=2)
    mask = np.stack([pair(masks), pair(flip(masks))])
    return dmat.astype(np.float32), mask.astype(np.float32)


def _hgrn_scan_kernel(q_ref, v_ref, zf_ref, zb_ref, g_ref, lbl_ref, nw_ref, dm_ref, mk_ref,
                      yc_ref, yx_ref, o_ref, oi_ref, qx_ref, ke_ref, r_ref, qf_ref, kf_ref, x_ref, u_ref, dec_ref,
                      *, layer, n_ctx):
    c = HGRN_CHUNK
    hd = HGRN_HEAD_DIM
    t_all = q_ref.shape[1]
    n_chunks = t_all // c
    n_ctx_chunks = n_ctx // c
    n_pairs = mk_ref.shape[1]
    n_blocks = dm_ref.shape[1] // c

    def lower_bound(direction):
        rows = [lbl_ref[direction, j:j + 1, :] for j in range(lbl_ref.shape[1])]
        mx = functools.reduce(jnp.maximum, rows)
        e = [jnp.exp(r - mx) for r in rows]
        tot = functools.reduce(lambda a, b_: a + b_, e)
        p = [ei / tot for ei in e]
        acc = p[0]
        for j in range(1, layer + 1):
            acc = acc + p[j]
        return acc - p[0]

    lb = (lower_bound(0), lower_bound(1))
    z_refs = (zf_ref, zb_ref)
    end_row = (c - 1, 0)
    zeros_k = jnp.zeros((c, hd), BF16)

    def decays(i, slot):
        rows2 = pl.ds(pl.multiple_of(i * 2 * c, 2 * c), 2 * c)
        q = _silu(q_ref[0, rows2, :])
        qf_ref[slot] = q.astype(BF16)
        for d in range(2):
            f = lb[d] + (1.0 - lb[d]) * _sigmoid(z_refs[d][0, rows2, :])
            k = 1.0 - f
            kf_ref[slot, d] = k.astype(BF16)
            g = jnp.log(f) * LOG2E
            hi = g.astype(BF16)
            r1 = g - hi.astype(F32)
            mid = r1.astype(BF16)
            lo = (r1 - mid.astype(F32)).astype(BF16)
            side = lambda a: jnp.concatenate([a[:c], a[c:]], axis=1)
            x2 = jnp.exp2(_dot(dm_ref[d], jnp.concatenate([side(hi), side(mid), side(lo)], axis=0)))
            for cc in range(2):
                xs = x2[:, cc * hd:(cc + 1) * hd]
                rows = pl.ds(pl.multiple_of(i * 2 * c + cc * c, c), c)
                x_ref[slot, d, cc] = xs[c:(n_blocks - 1) * c].astype(BF16)
                qx_ref[d, rows, :] = (q[cc * c:(cc + 1) * c] * xs[0:c]).astype(BF16)
                ke_ref[d, rows, :] = (k[cc * c:(cc + 1) * c] * xs[(n_blocks - 1) * c:]).astype(BF16)
                dec_ref[d, 2 * i + cc] = jnp.broadcast_to(xs[end_row[d]:end_row[d] + 1], (8, hd))

    def interactions(i, slot):
        for cc in range(2):
            rows = pl.ds(pl.multiple_of(i * 2 * c + cc * c, c), c)
            qc = qf_ref[slot, cc * c:(cc + 1) * c, :]
            for d in range(2):
                kc = kf_ref[slot, d, cc * c:(cc + 1) * c, :]
                xs = x_ref[slot, d, cc]
                diag = jnp.sum(qc.astype(F32) * kc.astype(F32), axis=-1, keepdims=True)
                r = mk_ref[d, n_pairs - 1] * diag
                for p in range(n_pairs - 1):
                    xa = xs[2 * p * c:(2 * p + 1) * c]
                    xb = xs[(2 * p + 1) * c:(2 * p + 2) * c]
                    lhs = jnp.concatenate([qc * xa, qc * xb], axis=1)
                    rhs = jnp.concatenate([jnp.concatenate([kc * xa, zeros_k], axis=1),
                                           jnp.concatenate([zeros_k, kc * xb], axis=1)], axis=0)
                    r = r + mk_ref[d, p] * _dot_nt(lhs, rhs)
                r_ref[d, rows, :] = r.astype(BF16)

    def values(ci, carry):
        rows = pl.ds(pl.multiple_of(ci * c, c), c)
        vc = v_ref[0, rows, :].astype(BF16)
        oo = _dot(jnp.concatenate([r_ref[0, rows, :], r_ref[1, rows, :]], axis=0),
                  jnp.concatenate([vc, vc], axis=0))
        o_ref[0, rows, :] = oo[:c]
        o_ref[1, rows, :] = oo[c:]
        uu = _dot_tn(vc, jnp.concatenate([ke_ref[0, rows, :], ke_ref[1, rows, :]], axis=1))
        u_ref[0, ci] = uu[:, :hd]
        u_ref[1, ci] = uu[:, hd:]
        return carry

    n_cpairs = n_chunks // 2

    def pipelined(k, carry):
        decays(2 * k + 1, 1)
        interactions(2 * k, 0)
        decays(2 * k + 2, 0)
        interactions(2 * k + 1, 1)
        return carry

    decays(0, 0)
    lax.fori_loop(0, n_cpairs // 2 - 1, pipelined, 0, unroll=True)
    decays(n_cpairs - 1, 1)
    interactions(n_cpairs - 2, 0)
    interactions(n_cpairs - 1, 1)
    lax.fori_loop(0, n_chunks, values, 0, unroll=True)

    def inter(i, states):
        cb = jnp.where(i < n_ctx_chunks, n_ctx_chunks - 1 - i, n_chunks - 1 - (i - n_ctx_chunks))
        new_states = []
        for d, ci in ((0, i), (1, cb)):
            rows = pl.ds(pl.multiple_of(ci * c, c), c)
            st = states[d]
            oi_ref[d, rows, :] = _dot_nt(qx_ref[d, rows, :], st.astype(BF16))
            new_states.append(st * dec_ref[d, ci][0:1] + u_ref[d, ci])
        return tuple(new_states)

    zero_state = jnp.zeros((hd, hd), F32)
    lax.fori_loop(0, n_chunks, inter, (zero_state, zero_state), unroll=True)

    nw = nw_ref[...]

    def finish(dst_ref, first_row):
        def tile(j, carry):
            rows = pl.ds(pl.multiple_of(first_row + j * TOKEN_TILE, TOKEN_TILE), TOKEN_TILE)
            o = (o_ref[0, rows, :] + oi_ref[0, rows, :]) + (o_ref[1, rows, :] + oi_ref[1, rows, :])
            o = o * lax.rsqrt(jnp.mean(o * o, axis=-1, keepdims=True) + RMS_EPS)
            gate = _silu(g_ref[0, rows, :])
            out_rows = pl.ds(pl.multiple_of(j * TOKEN_TILE, TOKEN_TILE), TOKEN_TILE)
            dst_ref[0, out_rows, :] = (o * nw * gate).astype(dst_ref.dtype)
            return carry

        lax.fori_loop(0, dst_ref.shape[1] // TOKEN_TILE, tile, 0, unroll=4)

    finish(yc_ref, 0)
    finish(yx_ref, n_ctx)


def _hgrn_scan(proj, lb_logits, norm_w_all, j_w, *, layer, n_ctx):
    b, t, n = proj.shape
    d = n // 5
    hd = HGRN_HEAD_DIM
    heads = d // hd
    n_chunks = t // HGRN_CHUNK
    assert n_chunks % 4 == 0
    dmat, mask = _hgrn_constants()
    part = lambda p: pl.BlockSpec((1, t, hd), lambda i, h: (i, 0, p * heads + h))
    kern = functools.partial(_hgrn_scan_kernel, layer=layer, n_ctx=n_ctx)
    return pl.pallas_call(
        kern,
        grid=(b, heads),
        in_specs=[
            part(0), part(1), part(2), part(3), part(4),
            pl.BlockSpec((2, lb_logits.shape[1], hd), lambda i, h: (0, 0, h)),
            pl.BlockSpec((None, 1, hd), lambda i, h: (j_w, 0, h)),
            _resident(dmat.shape),
            _resident(mask.shape),
        ],
        out_specs=[pl.BlockSpec((1, n_ctx, hd), lambda i, h: (i, 0, h)),
                   pl.BlockSpec((1, t - n_ctx, hd), lambda i, h: (i, 0, h))],
        out_shape=[jax.ShapeDtypeStruct((b, n_ctx, d), BF16),
                   jax.ShapeDtypeStruct((b, t - n_ctx, d), BF16)],
        scratch_shapes=[
            pltpu.VMEM((2, t, hd), F32),
            pltpu.VMEM((2, t, hd), F32),
            pltpu.VMEM((2, t, hd), BF16),
            pltpu.VMEM((2, t, hd), BF16),
            pltpu.VMEM((2, t, hd), BF16),
            pltpu.VMEM((2, 2 * HGRN_CHUNK, hd), BF16),
            pltpu.VMEM((2, 2, 2 * HGRN_CHUNK, hd), BF16),
            pltpu.VMEM((2, 2, 2, (dmat.shape[1] // HGRN_CHUNK - 2) * HGRN_CHUNK, hd), BF16),
            pltpu.VMEM((2, n_chunks, hd, hd), F32),
            pltpu.VMEM((2, n_chunks, 8, hd), F32),
        ],
        compiler_params=pltpu.CompilerParams(
            dimension_semantics=("arbitrary", "arbitrary"), vmem_limit_bytes=VMEM_LIMIT),
        name="hgrn_scan",
    )(proj, proj, proj, proj, proj, lb_logits, norm_w_all.reshape(norm_w_all.shape[0], 1, d),
      jnp.asarray(dmat, BF16), jnp.asarray(mask, F32))


def _na_bias_table(rpb):
    qcol = np.arange(GRID_W)[:, None]
    kcol = np.arange(GRID_W)[None, :]
    win = np.clip(qcol - NA_COLS // 2, 0, GRID_W - NA_COLS)
    ok = (kcol >= win) & (kcol < win + NA_COLS)
    dcol = np.arange(2 * NA_COLS - 1)[:, None, None]
    select = ((kcol - qcol + NA_COLS - 1)[None] == dcol) & ok[None]
    per_row = jnp.einsum('hrc,cqk->hrqk', rpb.astype(F32), jnp.asarray(select, F32),
                         precision=lax.Precision.HIGHEST)
    per_row = jnp.where(jnp.asarray(ok)[None, None], per_row * LOG2E, NEG_BIG)
    return jnp.concatenate([per_row[:, :-1], per_row[:, 1:]], axis=-1)


def _na_kernel(q_ref, k_ref, v_ref, bias_ref, yc_ref, yx_ref, qs_ref, s_ref, p_ref, l_ref, *, n_ctx, rows):
    w = GRID_W
    kr = min(NA_ROWS, rows)
    n_loc = kr * w
    grp = NA_ROW_GROUP
    lanes = 2 * NA_HEAD_DIM
    first = lax.broadcasted_iota(jnp.int32, (1, lanes), 1) < NA_HEAD_DIM

    def split_heads(qv):
        zero = jnp.zeros_like(qv)
        return jnp.concatenate([jnp.where(first, qv, zero), jnp.where(first, zero, qv)], axis=0)

    def merge_heads(o, n):
        return jnp.where(first, o[:n], o[n:])

    def window(r):
        rs = jnp.clip(r - kr // 2, 0, rows - kr)
        return r - rs, pl.ds(pl.multiple_of(n_ctx + rs * w, w), n_loc)

    kc = k_ref[0, 0:n_ctx, :]
    vc = v_ref[0, 0:n_ctx, :]

    s = _dot_nt(split_heads(q_ref[0, 0:n_ctx, :]), kc)
    p = jnp.exp2(s - jnp.max(s, axis=-1, keepdims=True))
    o = _dot(p.astype(BF16), vc) / jnp.sum(p, axis=-1, keepdims=True)
    yc_ref[0] = merge_heads(o, n_ctx).astype(yc_ref.dtype)

    def prep(r, carry):
        qs_ref[r] = split_heads(q_ref[0, pl.ds(pl.multiple_of(n_ctx + r * w, w), w), :])
        return carry

    def scores(g, slot):
        for rr in range(grp):
            case, keys = window(g * grp + rr)
            head_bias = lambda hh: jnp.concatenate(
                [bias_ref[hh, NA_ROWS - 1 - case + a] for a in range(0, kr, 2)], axis=1)
            bias = jnp.concatenate([head_bias(0), head_bias(1)], axis=0)
            s_ref[slot, rr, :, 0:n_loc] = _dot_nt(qs_ref[g * grp + rr], k_ref[0, keys, :]) + bias
        qg = qs_ref[pl.ds(g * grp, grp)].reshape(grp * 2 * w, lanes)
        s_ref[slot, :, :, n_loc:] = _dot_nt(qg, kc).reshape(grp, 2 * w, n_ctx)

    def softmax(slot):
        for rr in range(grp):
            sr = s_ref[slot, rr]
            tiles = [sr[:, i * lanes:(i + 1) * lanes] for i in range(sr.shape[1] // lanes)]
            m = jnp.max(functools.reduce(jnp.maximum, tiles), axis=-1, keepdims=True)
            pr = [jnp.exp2(tile - m) for tile in tiles]
            den = jnp.sum(functools.reduce(lambda u, v_: u + v_, pr), axis=-1, keepdims=True)
            l_ref[slot, rr] = jnp.broadcast_to(den, (2 * w, lanes))
            p_ref[slot, rr] = jnp.concatenate(pr, axis=1).astype(BF16)

    def values(g, slot):
        pg = p_ref[slot, :, :, n_loc:].reshape(grp * 2 * w, n_ctx)
        o_ctx = _dot(pg, vc).reshape(grp, 2 * w, lanes)
        for rr in range(grp):
            _, keys = window(g * grp + rr)
            orow = (_dot(p_ref[slot, rr, :, 0:n_loc], v_ref[0, keys, :]) + o_ctx[rr]) / l_ref[slot, rr]
            out_rows = pl.ds(pl.multiple_of((g * grp + rr) * w, w), w)
            yx_ref[0, out_rows, :] = merge_heads(orow, w).astype(yx_ref.dtype)

    n_groups = rows // grp

    def pipelined(k, carry):
        g = 2 * k
        scores(g + 2, 0)
        softmax(1)
        values(g, 0)
        scores(g + 3, 1)
        softmax(0)
        values(g + 1, 1)
        return carry

    lax.fori_loop(0, rows, prep, 0, unroll=8)
    scores(0, 0)
    scores(1, 1)
    softmax(0)
    lax.fori_loop(0, n_groups // 2 - 1, pipelined, 0, unroll=True)
    softmax(1)
    values(n_groups - 2, 0)
    values(n_groups - 1, 1)


def _na_attention(qkv, bias, *, n_ctx):
    b, t, n = qkv.shape
    d = n // 3
    lanes = 2 * NA_HEAD_DIM
    pairs = d // lanes
    rows = (t - n_ctx) // GRID_W
    assert rows % NA_ROW_GROUP == 0 and rows >= NA_ROWS and NA_ROWS % 2 == 0
    n_keys = NA_ROWS * GRID_W + n_ctx
    part = lambda p: pl.BlockSpec((1, t, lanes), lambda h, i: (i, 0, p * pairs + h))
    kern = functools.partial(_na_kernel, n_ctx=n_ctx, rows=rows)
    return pl.pallas_call(
        kern,
        grid=(pairs, b),
        in_specs=[
            part(0), part(1), part(2),
            pl.BlockSpec((2,) + bias.shape[1:], lambda h, i: (h, 0, 0, 0)),
        ],
        out_specs=[pl.BlockSpec((1, n_ctx, lanes), lambda h, i: (i, 0, h)),
                   pl.BlockSpec((1, t - n_ctx, lanes), lambda h, i: (i, 0, h))],
        out_shape=[jax.ShapeDtypeStruct((b, n_ctx, d), BF16),
                   jax.ShapeDtypeStruct((b, t - n_ctx, d), BF16)],
        scratch_shapes=[
            pltpu.VMEM((rows, 2 * GRID_W, lanes), BF16),
            pltpu.VMEM((2, NA_ROW_GROUP, 2 * GRID_W, n_keys), F32),
            pltpu.VMEM((2, NA_ROW_GROUP, 2 * GRID_W, n_keys), BF16),
            pltpu.VMEM((2, NA_ROW_GROUP, 2 * GRID_W, lanes), F32),
        ],
        compiler_params=pltpu.CompilerParams(
            dimension_semantics=("arbitrary", "arbitrary"), vmem_limit_bytes=VMEM_LIMIT),
        name="na_attention",
    )(qkv, qkv, qkv, bias)


def kernel(x, c, ctx, c_ctx, mod_w, mod_b, ln_g, ln_b, ffn_w_in, ffn_w_out, pool_w, pool_scale,
           hgrn_w_in, hgrn_lb_logits, hgrn_norm_w, hgrn_w_out, na_w_qkv, na_rpb, na_w_out):
    b, seq, d = x.shape
    n_ctx = ctx.shape[1]
    depth = mod_w.shape[0]
    assert n_ctx % TOKEN_TILE == 0 and seq % TOKEN_TILE == 0 and b < MOD_ROWS
    assert seq % GRID_W == 0 and n_ctx % HGRN_CHUNK == 0 and seq % HGRN_CHUNK == 0

    cc = jnp.concatenate([c, c_ctx[None, :], jnp.zeros((MOD_ROWS - b - 1, d), c.dtype)], axis=0)
    mod_all = _modulation(cc, mod_w, mod_b).reshape(depth, MOD_ROWS, N_MOD, d)
    w_in_all = ffn_w_in.astype(BF16)
    hgrn_w_in_all = hgrn_w_in.astype(BF16)
    hgrn_w_out_all = hgrn_w_out.astype(BF16)
    na_w_qkv_all = na_w_qkv.astype(BF16)
    na_w_out_all = na_w_out.astype(BF16)

    for i in range(depth):
        last = i == depth - 1
        kind = i % N_MIXERS
        j = i // N_MIXERS
        w_mix_all = None
        if kind == 0:
            y_ctx, y_x = _pool_mixer(None if last else ctx, x, mod_all, i, pool_w, pool_scale, j)
        elif kind == 1:
            proj = _modulated_proj(ctx, x, mod_all, i, hgrn_w_in_all, j, F32, n_chunk=d)
            y_ctx, y_x = _hgrn_scan(proj, hgrn_lb_logits, hgrn_norm_w, j, layer=i, n_ctx=n_ctx)
            w_mix_all = hgrn_w_out_all
        else:
            qkv = _modulated_proj(ctx, x, mod_all, i, na_w_qkv_all, j, BF16, n_chunk=d,
                                  first_chunk_scale=NA_HEAD_DIM ** -0.5 * LOG2E)
            y_ctx, y_x = _na_attention(qkv, _na_bias_table(na_rpb[j]), n_ctx=n_ctx)
            w_mix_all = na_w_out_all
        ctx, x = _ffn_block(None if last else ctx, x, y_ctx, y_x, mod_all, i, ln_g, ln_b,
                            w_mix_all, j, w_in_all, ffn_w_out)
    return x
```
